```python
import jax, jax.numpy as jnp
from jax import lax
import numpy as np

D_MODEL = 2048
BATCH = 4
SEQ = 2048
DEPTH = 1
DEC_BATCH = 128
DEC_SEQ = 8
PAST_LEN = 16384
PAGE_SIZE = 128

LRU_WIDTH = D_MODEL // 2
SC_WIDTH = D_MODEL - LRU_WIDTH
LRU_HEADS = 16
LRU_HEAD_DIM = LRU_WIDTH // LRU_HEADS
LRU_CONV = 4
LRU_C = 8.0
SC_GROUPS = 16
SC_CONV = 3
IN_COLS = 2 * LRU_WIDTH + 3 * SC_WIDTH
PEER_HEADS = 8
N_KEYS = 128
N_EXPERTS = N_KEYS * N_KEYS
PEER_TOPK = 16
QUERY_DIM = 256
HALF_DIM = QUERY_DIM // 2
PEER_BLOCK = 128
EPS = 1e-6

kernel_name = "hymba_rglru_shortconv_peer_adaln_step"


def rmsnorm(x, g):
    xf = x.astype(jnp.float32)
    y = xf * lax.rsqrt(jnp.mean(xf * xf, axis=-1, keepdims=True) + EPS)
    return (y * g.astype(jnp.float32)).astype(x.dtype)


def modulate(h, shift, scale):
    return h * (1 + scale[:, None, :]) + shift[:, None, :]


def causal_dwconv(x, buf, w):
    width = w.shape[0]
    s = x.shape[1]
    xp = jnp.concatenate([buf.astype(x.dtype), x], axis=1)
    out = w[0] * xp[:, 0:s]
    for k in range(1, width):
        out = out + w[k] * xp[:, k:k + s]
    return out, xp[:, -(width - 1):]


def rglru(x, h0, w_a, b_a, w_x, b_x, lam):
    b, s, _ = x.shape
    xh = x.reshape(b, s, LRU_HEADS, LRU_HEAD_DIM)
    r = jax.nn.sigmoid((jnp.einsum('bshi,hij->bshj', xh, w_a).reshape(b, s, LRU_WIDTH) + b_a).astype(jnp.float32))
    i = jax.nn.sigmoid((jnp.einsum('bshi,hij->bshj', xh, w_x).reshape(b, s, LRU_WIDTH) + b_x).astype(jnp.float32))
    log_a = -LRU_C * r * jax.nn.softplus(-lam.astype(jnp.float32))
    a = jnp.exp(log_a)
    u = jnp.sqrt(-jnp.expm1(2.0 * log_a)) * (i * x.astype(jnp.float32))

    def step(h, au):
        a_t, u_t = au
        h = a_t * h + u_t
        return h, h

    h_last, hs = lax.scan(step, h0.astype(jnp.float32),
                          (jnp.swapaxes(a, 0, 1), jnp.swapaxes(u, 0, 1)))
    return jnp.swapaxes(hs, 0, 1).astype(x.dtype), h_last.astype(x.dtype)


def peer(h, w_q, sub_keys, expert_u, expert_v):
    b, s, d = h.shape
    t = h.reshape(-1, d)
    n_tok = t.shape[0]
    nb = -(-n_tok // PEER_BLOCK)
    t = jnp.pad(t, ((0, nb * PEER_BLOCK - n_tok), (0, 0)))

    def block(xb):
        q = (xb @ w_q).reshape(PEER_BLOCK, PEER_HEADS, 2, HALF_DIM)
        sc = jnp.einsum('thpk,pnk->thpn', q, sub_keys).astype(jnp.float32)
        v1, i1 = lax.top_k(sc[:, :, 0], PEER_TOPK)
        v2, i2 = lax.top_k(sc[:, :, 1], PEER_TOPK)
        cand = (v1[..., :, None] + v2[..., None, :]).reshape(PEER_BLOCK, PEER_HEADS, PEER_TOPK * PEER_TOPK)
        cidx = (i1[..., :, None] * N_KEYS + i2[..., None, :]).reshape(PEER_BLOCK, PEER_HEADS, PEER_TOPK * PEER_TOPK)
        top, pos = lax.top_k(cand, PEER_TOPK)
        eidx = jnp.take_along_axis(cidx, pos, axis=-1)
        g = jax.nn.softmax(top, axis=-1).astype(xb.dtype)
        act = jax.nn.gelu(jnp.einsum('td,thkd->thk', xb, expert_u[eidx])) * g
        return jnp.einsum('thk,thkd->td', act, expert_v[eidx])

    out = lax.map(block, t.reshape(nb, PEER_BLOCK, d)).reshape(-1, d)[:n_tok]
    return out.reshape(b, s, d)


def layer(x, c, h0, lru_buf, sc_buf, w_ada, b_ada, norm1_g, norm2_g, w_in,
          lru_conv_w, lru_conv_b, lru_wa, lru_ba, lru_wx, lru_bx, lru_lambda,
          sconv_w, gnorm_lru_g, gnorm_sc_g, w_out, peer_wq, peer_sub_keys, peer_u, peer_v):
    mod = (jax.nn.silu(c) @ w_ada + b_ada).reshape(c.shape[0], 6, D_MODEL)
    shift1, scale1, gate1, shift2, scale2, gate2 = [mod[:, k] for k in range(6)]

    h = modulate(rmsnorm(x, norm1_g), shift1, scale1)
    proj = h @ w_in
    x_lru, y_gate, sc_b, sc_c, sc_x = jnp.split(
        proj, [LRU_WIDTH, 2 * LRU_WIDTH, 2 * LRU_WIDTH + SC_WIDTH, 2 * LRU_WIDTH + 2 * SC_WIDTH], axis=-1)

    xc, new_lru_buf = causal_dwconv(x_lru, lru_buf, lru_conv_w)
    rec, h_last = rglru(xc + lru_conv_b, h0, lru_wa, lru_ba, lru_wx, lru_bx, lru_lambda)
    out_lru = rec * jax.nn.gelu(y_gate)

    conv_out, new_sc_buf = causal_dwconv(sc_c * sc_x, sc_buf, sconv_w)
    out_sc = sc_b * conv_out

    mix = jnp.concatenate([rmsnorm(out_lru, gnorm_lru_g), rmsnorm(out_sc, gnorm_sc_g)], axis=-1) @ w_out
    x = x + gate1[:, None, :] * mix

    h2 = modulate(rmsnorm(x, norm2_g), shift2, scale2)
    x = x + gate2[:, None, :] * peer(h2, peer_wq, peer_sub_keys, peer_u, peer_v)
    return x, h_last, new_lru_buf, new_sc_buf


def setup_inputs(seed: int = 0) -> dict:
    key = jax.random.key(seed)
    ks = jax.random.split(key, 32)
    f32 = jnp.float32
    nrm = lambda k, shape, s: jax.random.normal(k, shape, f32) * s
    d = D_MODEL
    a0 = jax.random.uniform(ks[20], (DEPTH, LRU_WIDTH), f32, 0.9, 0.999)
    sig = a0 ** (1.0 / LRU_C)
    return {
        "x_prompt": nrm(ks[0], (BATCH, SEQ, d), 1.0),
        "x_sample": nrm(ks[1], (DEC_BATCH, DEC_SEQ, d), 1.0),
        "c_prompt": nrm(ks[2], (BATCH, d), 1.0),
        "c_sample": nrm(ks[3], (DEC_BATCH, d), 1.0),
        "state_lru_h": nrm(ks[4], (DEPTH, DEC_BATCH, LRU_WIDTH), 0.5),
        "state_lru_conv": nrm(ks[5], (DEPTH, DEC_BATCH, LRU_CONV - 1, LRU_WIDTH), 1.0),
        "state_sconv": nrm(ks[6], (DEPTH, DEC_BATCH, SC_CONV - 1, SC_WIDTH), 1.0),
        "w_ada": nrm(ks[7], (DEPTH, d, 6 * d), 0.3 * d ** -0.5),
        "b_ada": nrm(ks[8], (DEPTH, 6 * d), 0.01),
        "norm1_g": 1.0 + nrm(ks[9], (DEPTH, d), 0.02),
        "norm2_g": 1.0 + nrm(ks[10], (DEPTH, d), 0.02),
        "w_in": nrm(ks[11], (DEPTH, d, IN_COLS), d ** -0.5),
        "lru_conv_w": nrm(ks[12], (DEPTH, LRU_CONV, LRU_WIDTH), LRU_CONV ** -0.5),
        "lru_conv_b": nrm(ks[13], (DEPTH, LRU_WIDTH), 0.01),
        "lru_wa": nrm(ks[14], (DEPTH, LRU_HEADS, LRU_HEAD_DIM, LRU_HEAD_DIM), LRU_HEAD_DIM ** -0.5),
        "lru_ba": nrm(ks[15], (DEPTH, LRU_WIDTH), 0.01),
        "lru_wx": nrm(ks[16], (DEPTH, LRU_HEADS, LRU_HEAD_DIM, LRU_HEAD_DIM), LRU_HEAD_DIM ** -0.5),
        "lru_bx": nrm(ks[17], (DEPTH, LRU_WIDTH), 0.01),
        "lru_lambda": jnp.log(sig / (1.0 - sig)),
        "sconv_w": nrm(ks[18], (DEPTH, SC_CONV, SC_WIDTH), SC_CONV ** -0.5),
        "gnorm_lru_g": 1.0 + nrm(ks[19], (DEPTH, LRU_WIDTH), 0.02),
        "gnorm_sc_g": 1.0 + nrm(ks[21], (DEPTH, SC_WIDTH), 0.02),
        "w_out": nrm(ks[22], (DEPTH, d, d), d ** -0.5),
        "peer_wq": nrm(ks[23], (DEPTH, d, PEER_HEADS * QUERY_DIM), d ** -0.5),
        "peer_sub_keys": nrm(ks[24], (DEPTH, 2, N_KEYS, HALF_DIM), HALF_DIM ** -0.5),
        "peer_u": nrm(ks[25], (DEPTH, N_EXPERTS, d), d ** -0.5),
        "peer_v": nrm(ks[26], (DEPTH, N_EXPERTS, d), 0.3),
        "final_g": 1.0 + nrm(ks[27], (d,), 0.02),
    }


def reference(x_prompt, x_sample, c_prompt, c_sample, state_lru_h, state_lru_conv, state_sconv,
              w_ada, b_ada, norm1_g, norm2_g, w_in, lru_conv_w, lru_conv_b, lru_wa, lru_ba,
              lru_wx, lru_bx, lru_lambda, sconv_w, gnorm_lru_g, gnorm_sc_g, w_out,
              peer_wq, peer_sub_keys, peer_u, peer_v, final_g):
    xp, xs = x_prompt, x_sample
    bp = x_prompt.shape[0]
    zero_h = jnp.zeros((bp, LRU_WIDTH), x_prompt.dtype)
    zero_lbuf = jnp.zeros((bp, LRU_CONV - 1, LRU_WIDTH), x_prompt.dtype)
    zero_sbuf = jnp.zeros((bp, SC_CONV - 1, SC_WIDTH), x_prompt.dtype)
    hp_l, lbp_l, sbp_l, hs_l, lbs_l, sbs_l = [], [], [], [], [], []
    for l in range(DEPTH):
        lp = (w_ada[l], b_ada[l], norm1_g[l], norm2_g[l], w_in[l], lru_conv_w[l], lru_conv_b[l],
              lru_wa[l], lru_ba[l], lru_wx[l], lru_bx[l], lru_lambda[l], sconv_w[l],
              gnorm_lru_g[l], gnorm_sc_g[l], w_out[l], peer_wq[l], peer_sub_keys[l], peer_u[l], peer_v[l])
        xp, hp, lbp, sbp = layer(xp, c_prompt, zero_h, zero_lbuf, zero_sbuf, *lp)
        xs, hs, lbs, sbs = layer(xs, c_sample, state_lru_h[l], state_lru_conv[l], state_sconv[l], *lp)
        hp_l.append(hp); lbp_l.append(lbp); sbp_l.append(sbp)
        hs_l.append(hs); lbs_l.append(lbs); sbs_l.append(sbs)
    y_prompt = rmsnorm(xp, final_g)
    y_sample = rmsnorm(xs, final_g)
    return (y_prompt, y_sample,
            jnp.stack(hp_l), jnp.stack(lbp_l), jnp.stack(sbp_l),
            jnp.stack(hs_l), jnp.stack(lbs_l), jnp.stack(sbs_l))
```

```python
import functools

import jax
import jax.numpy as jnp
from jax import lax
from jax.experimental import pallas as pl
from jax.experimental.pallas import tpu as pltpu

F32 = jnp.float32
BF16 = jnp.bfloat16

EPS = 1e-6
LRU_C = 8.0
PEER_TOPK = 16

LANES = 128
SUBLANES = 8
MXU_DIM = 256
VMEM_BYTES = 64 * 1024 * 1024

PROMPT_TILE = 256
PEER_TOKENS = 512
PEER_EXPERTS = 1024
ADALN_COLS = 1024


def _dot(a, b):
    return jnp.dot(a, b, preferred_element_type=F32)


def _rms(x, g):
    ms = jnp.mean(x * x, axis=-1, keepdims=True)
    return x * lax.rsqrt(ms + EPS) * g


def _const_spec(shape):
    nd = len(shape)
    return pl.BlockSpec(shape, lambda *_: (0,) * nd, pipeline_mode=pl.Buffered(1))


def _params(sem, vmem_mb):
    return pltpu.CompilerParams(dimension_semantics=sem, vmem_limit_bytes=vmem_mb * 1024 * 1024)


def _adaln_kernel(c_ref, w_ref, b_ref, o_ref):
    sc = jax.nn.silu(c_ref[...]).astype(BF16)
    o_ref[...] = _dot(sc, w_ref[...].astype(BF16)) + b_ref[...]


def _adaln(c_all, w_ada, b_ada):
    m, d = c_all.shape
    n = w_ada.shape[1]
    tn = ADALN_COLS
    return pl.pallas_call(
        _adaln_kernel,
        grid=(n // tn,),
        in_specs=[pl.BlockSpec((m, d), lambda j: (0, 0)),
                  pl.BlockSpec((d, tn), lambda j: (0, j)),
                  pl.BlockSpec((1, tn), lambda j: (0, j))],
        out_specs=pl.BlockSpec((m, tn), lambda j: (0, j)),
        out_shape=jax.ShapeDtypeStruct((m, n), F32),
        compiler_params=_params(("arbitrary",), 40),
    )(c_all, w_ada, b_ada.reshape(1, n))


def _gate_pre(xcb, w_ref, b_ref):
    nblk = w_ref.shape[0]
    parts = [_dot(xcb[:, g * MXU_DIM:(g + 1) * MXU_DIM], w_ref[g]) for g in range(nblk)]
    return jnp.concatenate(parts, axis=1) + b_ref[...]


def _scan_rows(a, u, h_in):
    rows = a.shape[0]
    row = lax.broadcasted_iota(jnp.int32, (rows, 1), 0)
    acc_a = a
    acc_b = u + jnp.where(row == 0, a * h_in, 0.0)
    d = 1
    while d < rows:
        keep = row >= d
        sh_b = pltpu.roll(acc_b, d, axis=0)
        acc_b = acc_b + jnp.where(keep, acc_a * sh_b, 0.0)
        if 2 * d < rows:
            sh_a = pltpu.roll(acc_a, d, axis=0)
            acc_a = jnp.where(keep, acc_a * sh_a, acc_a)
        d *= 2
    return acc_b


def _mixer_core(x, mod_ref, n1g_ref, win_ref, cb_ref, wa_ref, ba_ref, wx_ref, bx_ref, lam_ref,
                gl_ref, gs_ref, wout_ref, lru_conv, lru_scan, sc_conv):
    w = cb_ref.shape[-1]
    v = gs_ref.shape[-1]
    shift1, scale1, gate1 = mod_ref[0], mod_ref[1], mod_ref[2]
    h = _rms(x, n1g_ref[...]) * (1.0 + scale1) + shift1
    hb = h.astype(BF16)

    x_lru = _dot(hb, win_ref[:, 0:w])
    xc = lru_conv(x_lru) + cb_ref[...]
    xcb = xc.astype(BF16)
    r = jax.nn.sigmoid(_gate_pre(xcb, wa_ref, ba_ref))
    i = jax.nn.sigmoid(_gate_pre(xcb, wx_ref, bx_ref))
    log_a = (-LRU_C) * r * jax.nn.softplus(-lam_ref[...])
    a = jnp.exp(log_a)
    u = jnp.sqrt(jnp.tanh(-log_a) * (1.0 + a * a)) * (i * xc)
    rec = lru_scan(a, u)
    y_gate = _dot(hb, win_ref[:, w:2 * w])
    out_lru = rec * jax.nn.gelu(y_gate)

    sc_b = _dot(hb, win_ref[:, 2 * w:2 * w + v])
    sc_c = _dot(hb, win_ref[:, 2 * w + v:2 * w + 2 * v])
    sc_x = _dot(hb, win_ref[:, 2 * w + 2 * v:2 * w + 3 * v])
    cx = sc_c * sc_x
    out_sc = sc_b * sc_conv(cx)

    n_l = _rms(out_lru, gl_ref[...]).astype(BF16)
    n_s = _rms(out_sc, gs_ref[...]).astype(BF16)
    mix = _dot(n_l, wout_ref[0:w, :]) + _dot(n_s, wout_ref[w:w + v, :])
    return x + gate1 * mix, x_lru, rec, cx


def _mixer_prompt_kernel(x_ref, mod_ref, n1g_ref, win_ref, cw_ref, cb_ref, wa_ref, ba_ref, wx_ref,
                         bx_ref, lam_ref, sw_ref, gl_ref, gs_ref, wout_ref,
                         x1_ref, hl_ref, lb_ref, sb_ref, *, tiles_per_seq):
    rows = x_ref.shape[0]
    hdr = SUBLANES

    @pl.when(pl.program_id(0) % tiles_per_seq == 0)
    def _():
        hl_ref[...] = jnp.zeros_like(hl_ref)
        lb_ref[...] = jnp.zeros_like(lb_ref)
        sb_ref[...] = jnp.zeros_like(sb_ref)

    def conv(cur, tail_ref, w_ref):
        width = w_ref.shape[0]
        ext = jnp.concatenate([tail_ref[...], cur], axis=0)
        out = w_ref[width - 1:width, :] * cur
        for d in range(1, width):
            out = out + w_ref[width - 1 - d:width - d, :] * pltpu.roll(ext, d, axis=0)[hdr:]
        return out

    new_x, x_lru, rec, cx = _mixer_core(
        x_ref[...], mod_ref, n1g_ref, win_ref, cb_ref, wa_ref, ba_ref, wx_ref, bx_ref, lam_ref,
        gl_ref, gs_ref, wout_ref,
        lru_conv=lambda cur: conv(cur, lb_ref, cw_ref),
        lru_scan=lambda a, u: _scan_rows(a, u, hl_ref[hdr - 1:hdr, :]),
        sc_conv=lambda cur: conv(cur, sb_ref, sw_ref))
    x1_ref[...] = new_x
    hl_ref[...] = rec[rows - hdr:, :]
    lb_ref[...] = x_lru[rows - hdr:, :]
    sb_ref[...] = cx[rows - hdr:, :]


def _mixer_sample_kernel(x_ref, mod_ref, n1g_ref, win_ref, cw_ref, cb_ref, wa_ref, ba_ref, wx_ref,
                         bx_ref, lam_ref, sw_ref, gl_ref, gs_ref, wout_ref, h0_ref, lb0_ref, sb0_ref,
                         x1_ref, h_ref, lb_ref, sb_ref):
    @pl.when(pl.program_id(0) == 0)
    def _():
        h_ref[...] = h0_ref[...]
        lb_ref[...] = lb0_ref[...]
        sb_ref[...] = sb0_ref[...]

    def conv(cur, hist_ref, w_ref):
        width = w_ref.shape[0]
        out = w_ref[width - 1:width, :] * cur
        for k in range(width - 1):
            out = out + w_ref[k:k + 1, :] * hist_ref[k]
        for k in range(width - 2):
            hist_ref[k] = hist_ref[k + 1]
        hist_ref[width - 2] = cur
        return out

    def scan(a, u):
        h = a * h_ref[...] + u
        h_ref[...] = h
        return h

    new_x, _, _, _ = _mixer_core(
        x_ref[...], mod_ref, n1g_ref, win_ref, cb_ref, wa_ref, ba_ref, wx_ref, bx_ref, lam_ref,
        gl_ref, gs_ref, wout_ref,
        lru_conv=lambda cur: conv(cur, lb_ref, cw_ref),
        lru_scan=scan,
        sc_conv=lambda cur: conv(cur, sb_ref, sw_ref))
    x1_ref[...] = new_x


def _mixer_weight_specs(lw):
    return [_const_spec(lw[k].shape) for k in
            ("norm1_g", "w_in", "lru_conv_w", "lru_conv_b", "wa_bd", "lru_ba", "wx_bd", "lru_bx",
             "lru_lambda", "sconv_w", "gnorm_lru_g", "gnorm_sc_g", "w_out")]


def _mixer_weight_args(lw):
    return [lw[k] for k in
            ("norm1_g", "w_in", "lru_conv_w", "lru_conv_b", "wa_bd", "lru_ba", "wx_bd", "lru_bx",
             "lru_lambda", "sconv_w", "gnorm_lru_g", "gnorm_sc_g", "w_out")]


def _mixer_prompt(x_flat, mod_p, lw, batch, seq):
    t, d = x_flat.shape
    w = lw["lru_conv_b"].shape[-1]
    v = lw["gnorm_sc_g"].shape[-1]
    r = PROMPT_TILE
    tps = seq // r
    state = lambda c: pl.BlockSpec((None, SUBLANES, c), lambda i: (i // tps, 0, 0))
    return pl.pallas_call(
        functools.partial(_mixer_prompt_kernel, tiles_per_seq=tps),
        grid=(t // r,),
        in_specs=[pl.BlockSpec((r, d), lambda i: (i, 0)),
                  pl.BlockSpec((6, None, 1, d), lambda i: (0, i // tps, 0, 0))] + _mixer_weight_specs(lw),
        out_specs=[pl.BlockSpec((r, d), lambda i: (i, 0)), state(w), state(w), state(v)],
        out_shape=[jax.ShapeDtypeStruct((t, d), F32),
                   jax.ShapeDtypeStruct((batch, SUBLANES, w), F32),
                   jax.ShapeDtypeStruct((batch, SUBLANES, w), F32),
                   jax.ShapeDtypeStruct((batch, SUBLANES, v), F32)],
        compiler_params=_params(("arbitrary",), 60),
    )(x_flat, mod_p, *_mixer_weight_args(lw))


def _mixer_sample(x_flat, mod_s, lw, h0, lb0, sb0, batch):
    t, d = x_flat.shape
    w = lw["lru_conv_b"].shape[-1]
    v = lw["gnorm_sc_g"].shape[-1]
    full = lambda shape: pl.BlockSpec(shape, lambda i: (0,) * len(shape))
    return pl.pallas_call(
        _mixer_sample_kernel,
        grid=(t // batch,),
        in_specs=[pl.BlockSpec((batch, d), lambda i: (i, 0)), _const_spec(mod_s.shape)]
        + _mixer_weight_specs(lw) + [_const_spec(h0.shape), _const_spec(lb0.shape), _const_spec(sb0.shape)],
        out_specs=[pl.BlockSpec((batch, d), lambda i: (i, 0)), full(h0.shape), full(lb0.shape), full(sb0.shape)],
        out_shape=[jax.ShapeDtypeStruct((t, d), F32),
                   jax.ShapeDtypeStruct(h0.shape, F32),
                   jax.ShapeDtypeStruct(lb0.shape, F32),
                   jax.ShapeDtypeStruct(sb0.shape, F32)],
        compiler_params=_params(("arbitrary",), 60),
    )(x_flat, mod_s, *_mixer_weight_args(lw), h0, lb0, sb0)


def _top_values(work, k):
    kp = -(-k // SUBLANES) * SUBLANES
    row = lax.broadcasted_iota(jnp.int32, (kp, 1), 0)
    out = jnp.full((kp, work.shape[1]), -jnp.inf, F32)
    for j in range(k):
        m = jnp.max(work, axis=0, keepdims=True)
        out = jnp.where(row == j, m, out)
        if j + 1 < k:
            work = jnp.where(work == m, -jnp.inf, work)
    return out


def _prep_kernel(x1_ref, mod_ref, n2g_ref, wq_ref, sk_ref, h2t_ref, sct_ref, st_ref):
    nk = sk_ref.shape[1]
    n_heads = wq_ref.shape[1] // (2 * nk)
    k = PEER_TOPK
    shift2, scale2 = mod_ref[3], mod_ref[4]
    h2 = _rms(x1_ref[...], n2g_ref[...]) * (1.0 + scale2) + shift2
    h2t_ref[...] = h2.T.astype(BF16)
    qb = _dot(h2.astype(BF16), wq_ref[...]).astype(BF16)
    nt = (((1,), (1,)), ((), ()))
    for hd in range(n_heads):
        s = []
        for p in range(2):
            c = 2 * hd + p
            sc = lax.dot_general(sk_ref[p], qb[:, c * nk:(c + 1) * nk], nt, preferred_element_type=F32)
            sct_ref[c * nk:(c + 1) * nk, :] = sc
            s.append(sc)
        v1 = _top_values(s[0], k)
        v2 = _top_values(s[1], k)
        blocks = [v1[0:1, :] + v2]
        for a in range(1, SUBLANES):
            blocks.append(v1[a:a + 1, :] + v2[0:SUBLANES, :])
        blocks.append(v1[SUBLANES:, :] + v2[0:1, :])
        top = _top_values(jnp.concatenate(blocks, axis=0), k + 1)
        m = top[0:1, :]
        row = lax.broadcasted_iota(jnp.int32, (top.shape[0], 1), 0)
        z = jnp.sum(jnp.where(row < k, jnp.exp(top - m), 0.0), axis=0, keepdims=True)
        tau = 0.5 * (top[k - 1:k, :] + top[k:k + 1, :])
        st_ref[4 * hd:4 * hd + 4, :] = jnp.concatenate([tau, v1[0:1, :], v2[0:1, :], 1.0 / z], axis=0)


def _peer_prep(x1_flat, mod, mod_spec, rows, lw):
    t, d = x1_flat.shape
    nq = lw["peer_wq"].shape[1]
    nk = lw["peer_sub_keys"].shape[1]
    n_heads = nq // (2 * nk)
    return pl.pallas_call(
        _prep_kernel,
        grid=(t // rows,),
        in_specs=[pl.BlockSpec((rows, d), lambda i: (i, 0)), mod_spec,
                  _const_spec(lw["norm2_g"].shape), _const_spec(lw["peer_wq"].shape),
                  _const_spec(lw["peer_sub_keys"].shape)],
        out_specs=[pl.BlockSpec((d, rows), lambda i: (0, i)),
                   pl.BlockSpec((nq, rows), lambda i: (0, i)),
                   pl.BlockSpec((4 * n_heads, rows), lambda i: (0, i))],
        out_shape=[jax.ShapeDtypeStruct((d, t), BF16),
                   jax.ShapeDtypeStruct((nq, t), F32),
                   jax.ShapeDtypeStruct((4 * n_heads, t), F32)],
        compiler_params=_params(("arbitrary",), 48),
    )(x1_flat, mod, lw["norm2_g"], lw["peer_wq"], lw["peer_sub_keys"])


def _peer_kernel(h2t_ref, sct_ref, st_ref, u_ref, vt_ref, out_ref,
                 th_ref, e1_ref, e2_ref, s_ref, a_ref, *, n_heads, nk):
    c = pl.program_id(1)
    ec = u_ref.shape[0]
    i_per_step = ec // nk
    jrows = 2 * SUBLANES
    j_groups = nk // jrows

    @pl.when(c == 0)
    def _():
        out_ref[...] = jnp.zeros_like(out_ref)
        for hd in range(n_heads):
            s1 = sct_ref[2 * hd * nk:(2 * hd + 1) * nk, :]
            s2 = sct_ref[(2 * hd + 1) * nk:(2 * hd + 2) * nk, :]
            tau = st_ref[4 * hd:4 * hd + 1, :]
            m1 = st_ref[4 * hd + 1:4 * hd + 2, :]
            m2 = st_ref[4 * hd + 2:4 * hd + 3, :]
            rz = st_ref[4 * hd + 3:4 * hd + 4, :]
            th_ref[hd * nk:(hd + 1) * nk, :] = tau - s1
            e1_ref[hd * nk:(hd + 1) * nk, :] = jnp.exp(s1 - m1) * rz
            e2_ref[hd * nk:(hd + 1) * nk, :] = jnp.exp(s2 - m2)

    s_ref[...] = _dot(u_ref[...], h2t_ref[...])

    def body(it, carry):
        il = it // j_groups
        jg = it % j_groups
        i = c * i_per_step + il
        row0 = pl.multiple_of(il * nk + jg * jrows, jrows)
        j0 = pl.multiple_of(jg * jrows, jrows)
        gate = jnp.zeros((jrows, s_ref.shape[1]), F32)
        for hd in range(n_heads):
            th = th_ref[pl.ds(hd * nk + i, 1), :]
            e1 = e1_ref[pl.ds(hd * nk + i, 1), :]
            s2 = sct_ref[pl.ds((2 * hd + 1) * nk + j0, jrows), :]
            e2 = e2_ref[pl.ds(hd * nk + j0, jrows), :]
            gate = gate + jnp.where(s2 >= th, e2 * e1, 0.0)
        act = jax.nn.gelu(s_ref[pl.ds(row0, jrows), :]) * gate
        a_ref[pl.ds(row0, jrows), :] = act.astype(BF16)
        return carry

    lax.fori_loop(0, i_per_step * j_groups, body, 0)
    out_ref[...] += _dot(vt_ref[...], a_ref[...])


def _peer_experts(h2t, sct, st, u_bf, vt_bf, n_heads, nk):
    d, t = h2t.shape
    ne = u_bf.shape[0]
    tb = min(PEER_TOKENS, t)
    ec = PEER_EXPERTS
    tok = lambda rows: pl.BlockSpec((rows, tb), lambda i, c: (0, i), pipeline_mode=pl.Buffered(1))
    return pl.pallas_call(
        functools.partial(_peer_kernel, n_heads=n_heads, nk=nk),
        grid=(t // tb, ne // ec),
        in_specs=[tok(d), tok(sct.shape[0]), tok(st.shape[0]),
                  pl.BlockSpec((ec, d), lambda i, c: (c, 0)),
                  pl.BlockSpec((d, ec), lambda i, c: (0, c))],
        out_specs=pl.BlockSpec((d, tb), lambda i, c: (0, i)),
        out_shape=jax.ShapeDtypeStruct((d, t), F32),
        scratch_shapes=[pltpu.VMEM((n_heads * nk, tb), F32),
                        pltpu.VMEM((n_heads * nk, tb), F32),
                        pltpu.VMEM((n_heads * nk, tb), F32),
                        pltpu.VMEM((ec, tb), F32),
                        pltpu.VMEM((ec, tb), BF16)],
        compiler_params=_params(("arbitrary", "arbitrary"), 52),
    )(h2t, sct, st, u_bf, vt_bf)


def _epilogue_kernel(pt_ref, x1_ref, mod_ref, fg_ref, y_ref, *, final_norm):
    xo = x1_ref[...] + mod_ref[5] * pt_ref[...].T
    y_ref[...] = _rms(xo, fg_ref[...]) if final_norm else xo


def _epilogue(peer_t, x1_flat, mod, mod_spec, rows, final_g, final_norm):
    t, d = x1_flat.shape
    return pl.pallas_call(
        functools.partial(_epilogue_kernel, final_norm=final_norm),
        grid=(t // rows,),
        in_specs=[pl.BlockSpec((d, rows), lambda i: (0, i)),
                  pl.BlockSpec((rows, d), lambda i: (i, 0)), mod_spec, _const_spec(final_g.shape)],
        out_specs=pl.BlockSpec((rows, d), lambda i: (i, 0)),
        out_shape=jax.ShapeDtypeStruct((t, d), F32),
        compiler_params=_params(("arbitrary",), 40),
    )(peer_t, x1_flat, mod, final_g)


def _block_diag(w):
    heads, hd, _ = w.shape
    per = MXU_DIM // hd
    wg = w.reshape(heads // per, per, hd, hd)
    eye = jnp.eye(per, dtype=w.dtype)
    return jnp.einsum("gpij,pq->gpiqj", wg, eye).reshape(heads // per, MXU_DIM, MXU_DIM).astype(BF16)


def kernel(x_prompt, x_sample, c_prompt, c_sample, state_lru_h, state_lru_conv, state_sconv, w_ada, b_ada, norm1_g, norm2_g, w_in, lru_conv_w, lru_conv_b, lru_wa, lru_ba, lru_wx, lru_bx, lru_lambda, sconv_w, gnorm_lru_g, gnorm_sc_g, w_out, peer_wq, peer_sub_keys, peer_u, peer_v, final_g):
    bp, seq, d = x_prompt.shape
    bs, sseq, _ = x_sample.shape
    depth = w_ada.shape[0]
    nk = peer_sub_keys.shape[2]
    n_heads = peer_wq.shape[2] // (2 * nk)
    tps = seq // PROMPT_TILE

    xp = x_prompt.reshape(bp * seq, d)
    xs = jnp.swapaxes(x_sample, 0, 1).reshape(sseq * bs, d)
    n_c = bp + bs
    pad = (-n_c) % SUBLANES
    c_all = jnp.concatenate([c_prompt, c_sample, jnp.zeros((pad, d), F32)], axis=0)
    fg = final_g.reshape(1, d)

    mod_p_spec = pl.BlockSpec((6, None, 1, d), lambda i: (0, i // tps, 0, 0))
    mod_s_spec = _const_spec((6, bs, d))

    outs = [[] for _ in range(6)]
    for l in range(depth):
        row = lambda a: a[l].reshape(1, -1)
        lw = dict(
            norm1_g=row(norm1_g), norm2_g=row(norm2_g), w_in=w_in[l].astype(BF16),
            lru_conv_w=lru_conv_w[l], lru_conv_b=row(lru_conv_b),
            wa_bd=_block_diag(lru_wa[l]), lru_ba=row(lru_ba),
            wx_bd=_block_diag(lru_wx[l]), lru_bx=row(lru_bx), lru_lambda=row(lru_lambda),
            sconv_w=sconv_w[l], gnorm_lru_g=row(gnorm_lru_g), gnorm_sc_g=row(gnorm_sc_g),
            w_out=w_out[l].astype(BF16), peer_wq=peer_wq[l].astype(BF16),
            peer_sub_keys=peer_sub_keys[l].astype(BF16))
        u_bf = peer_u[l].astype(BF16)
        vt_bf = peer_v[l].T.astype(BF16)

        mod = _adaln(c_all, w_ada[l], b_ada[l])
        mod_p = jnp.swapaxes(mod[:bp].reshape(bp, 6, d), 0, 1).reshape(6, bp, 1, d)
        mod_s = jnp.swapaxes(mod[bp:n_c].reshape(bs, 6, d), 0, 1)

        xp1, hl, lb, sb = _mixer_prompt(xp, mod_p, lw, bp, seq)
        h0 = state_lru_h[l]
        lb0 = jnp.swapaxes(state_lru_conv[l], 0, 1)
        sb0 = jnp.swapaxes(state_sconv[l], 0, 1)
        xs1, hs, lbs, sbs = _mixer_sample(xs, mod_s, lw, h0, lb0, sb0, bs)

        last = l == depth - 1
        new_x = []
        for x1, mod_g, spec, rows in ((xp1, mod_p, mod_p_spec, PROMPT_TILE), (xs1, mod_s, mod_s_spec, bs)):
            h2t, sct, st = _peer_prep(x1, mod_g, spec, rows, lw)
            peer_t = _peer_experts(h2t, sct, st, u_bf, vt_bf, n_heads, nk)
            new_x.append(_epilogue(peer_t, x1, mod_g, spec, rows, fg, last))
        xp, xs = new_x

        nl = lru_conv_w.shape[1] - 1
        ns = sconv_w.shape[1] - 1
        outs[0].append(hl[:, SUBLANES - 1, :])
        outs[1].append(lb[:, SUBLANES - nl:, :])
        outs[2].append(sb[:, SUBLANES - ns:, :])
        outs[3].append(hs)
        outs[4].append(jnp.swapaxes(lbs, 0, 1))
        outs[5].append(jnp.swapaxes(sbs, 0, 1))

    y_prompt = xp.reshape(bp, seq, d)
    y_sample = jnp.swapaxes(xs.reshape(sseq, bs, d), 0, 1)
    return (y_prompt, y_sample) + tuple(jnp.stack(o) for o in outs)
```

```python
import functools
import itertools

import jax
import jax.numpy as jnp
from jax import lax
from jax.experimental import pallas as pl
from jax.experimental.pallas import tpu as pltpu

F32 = jnp.float32
BF16 = jnp.bfloat16

EPS = 1e-6
LRU_C = 8.0
PEER_TOPK = 16

LANES = 128
SUBLANES = 8
MXU_DIM = 256
VMEM_BYTES = 64 * 1024 * 1024

PROMPT_TILE = 256
PEER_TOKENS = 512
PEER_EXPERTS = 1024
PEER_I_GROUP = 2
PEER_GATE_LANES = 256
ADALN_COLS = 1024


def _dot(a, b):
    return jnp.dot(a, b, preferred_element_type=F32)


def _rms(x, g):
    ms = jnp.mean(x * x, axis=-1, keepdims=True)
    return x * lax.rsqrt(ms + EPS) * g


def _const_spec(shape):
    nd = len(shape)
    return pl.BlockSpec(shape, lambda *_: (0,) * nd, pipeline_mode=pl.Buffered(1))


def _params(sem, vmem_mb):
    return pltpu.CompilerParams(dimension_semantics=sem, vmem_limit_bytes=vmem_mb * 1024 * 1024)


def _adaln_kernel(c_ref, w_ref, b_ref, o_ref):
    sc = jax.nn.silu(c_ref[...]).astype(BF16)
    o_ref[...] = _dot(sc, w_ref[...].astype(BF16)) + b_ref[...]


def _adaln(c_all, w_ada, b_ada):
    m, d = c_all.shape
    n = w_ada.shape[1]
    tn = ADALN_COLS
    return pl.pallas_call(
        _adaln_kernel,
        grid=(n // tn,),
        in_specs=[pl.BlockSpec((m, d), lambda j: (0, 0)),
                  pl.BlockSpec((d, tn), lambda j: (0, j)),
                  pl.BlockSpec((1, tn), lambda j: (0, j))],
        out_specs=pl.BlockSpec((m, tn), lambda j: (0, j)),
        out_shape=jax.ShapeDtypeStruct((m, n), F32),
        compiler_params=_params(("arbitrary",), 40),
    )(c_all, w_ada, b_ada.reshape(1, n))


def _gate_pre(xcb, w_ref, b_ref):
    nblk = w_ref.shape[0]
    parts = [_dot(xcb[:, g * MXU_DIM:(g + 1) * MXU_DIM], w_ref[g]) for g in range(nblk)]
    return jnp.concatenate(parts, axis=1) + b_ref[...]


def _scan_rows(a, u, h_in):
    rows = a.shape[0]
    row = lax.broadcasted_iota(jnp.int32, (rows, 1), 0)
    acc_a = a
    acc_b = u + jnp.where(row == 0, a * h_in, 0.0)
    d = 1
    while d < rows:
        keep = row >= d
        sh_b = pltpu.roll(acc_b, d, axis=0)
        acc_b = acc_b + jnp.where(keep, acc_a * sh_b, 0.0)
        if 2 * d < rows:
            sh_a = pltpu.roll(acc_a, d, axis=0)
            acc_a = jnp.where(keep, acc_a * sh_a, acc_a)
        d *= 2
    return acc_b


def _mixer_core(x, mod_ref, n1g_ref, win_ref, cb_ref, wa_ref, ba_ref, wx_ref, bx_ref, lam_ref,
                gl_ref, gs_ref, wout_ref, lru_conv, lru_scan, sc_conv):
    w = cb_ref.shape[-1]
    v = gs_ref.shape[-1]
    shift1, scale1, gate1 = mod_ref[0], mod_ref[1], mod_ref[2]
    h = _rms(x, n1g_ref[...]) * (1.0 + scale1) + shift1
    hb = h.astype(BF16)

    x_lru = _dot(hb, win_ref[:, 0:w])
    xc = lru_conv(x_lru) + cb_ref[...]
    xcb = xc.astype(BF16)
    r = jax.nn.sigmoid(_gate_pre(xcb, wa_ref, ba_ref))
    i = jax.nn.sigmoid(_gate_pre(xcb, wx_ref, bx_ref))
    log_a = (-LRU_C) * r * jax.nn.softplus(-lam_ref[...])
    a = jnp.exp(log_a)
    u = jnp.sqrt(jnp.tanh(-log_a) * (1.0 + a * a)) * (i * xc)
    rec = lru_scan(a, u)
    y_gate = _dot(hb, win_ref[:, w:2 * w])
    out_lru = rec * jax.nn.gelu(y_gate)

    sc_b = _dot(hb, win_ref[:, 2 * w:2 * w + v])
    sc_c = _dot(hb, win_ref[:, 2 * w + v:2 * w + 2 * v])
    sc_x = _dot(hb, win_ref[:, 2 * w + 2 * v:2 * w + 3 * v])
    cx = sc_c * sc_x
    out_sc = sc_b * sc_conv(cx)

    n_l = _rms(out_lru, gl_ref[...]).astype(BF16)
    n_s = _rms(out_sc, gs_ref[...]).astype(BF16)
    mix = _dot(n_l, wout_ref[0:w, :]) + _dot(n_s, wout_ref[w:w + v, :])
    return x + gate1 * mix, x_lru, rec, cx


def _mixer_prompt_kernel(x_ref, mod_ref, n1g_ref, win_ref, cw_ref, cb_ref, wa_ref, ba_ref, wx_ref,
                         bx_ref, lam_ref, sw_ref, gl_ref, gs_ref, wout_ref,
                         x1_ref, hl_ref, lb_ref, sb_ref, *, tiles_per_seq):
    rows = x_ref.shape[0]
    hdr = SUBLANES

    @pl.when(pl.program_id(0) % tiles_per_seq == 0)
    def _():
        hl_ref[...] = jnp.zeros_like(hl_ref)
        lb_ref[...] = jnp.zeros_like(lb_ref)
        sb_ref[...] = jnp.zeros_like(sb_ref)

    def conv(cur, tail_ref, w_ref):
        width = w_ref.shape[0]
        ext = jnp.concatenate([tail_ref[...], cur], axis=0)
        out = w_ref[width - 1:width, :] * cur
        for d in range(1, width):
            out = out + w_ref[width - 1 - d:width - d, :] * pltpu.roll(ext, d, axis=0)[hdr:]
        return out

    new_x, x_lru, rec, cx = _mixer_core(
        x_ref[...], mod_ref, n1g_ref, win_ref, cb_ref, wa_ref, ba_ref, wx_ref, bx_ref, lam_ref,
        gl_ref, gs_ref, wout_ref,
        lru_conv=lambda cur: conv(cur, lb_ref, cw_ref),
        lru_scan=lambda a, u: _scan_rows(a, u, hl_ref[hdr - 1:hdr, :]),
        sc_conv=lambda cur: conv(cur, sb_ref, sw_ref))
    x1_ref[...] = new_x
    hl_ref[...] = rec[rows - hdr:, :]
    lb_ref[...] = x_lru[rows - hdr:, :]
    sb_ref[...] = cx[rows - hdr:, :]


def _mixer_sample_kernel(x_ref, mod_ref, n1g_ref, win_ref, cw_ref, cb_ref, wa_ref, ba_ref, wx_ref,
                         bx_ref, lam_ref, sw_ref, gl_ref, gs_ref, wout_ref, h0_ref, lb0_ref, sb0_ref,
                         x1_ref, h_ref, lb_ref, sb_ref):
    @pl.when(pl.program_id(0) == 0)
    def _():
        h_ref[...] = h0_ref[...]
        lb_ref[...] = lb0_ref[...]
        sb_ref[...] = sb0_ref[...]

    def conv(cur, hist_ref, w_ref):
        width = w_ref.shape[0]
        out = w_ref[width - 1:width, :] * cur
        for k in range(width - 1):
            out = out + w_ref[k:k + 1, :] * hist_ref[k]
        for k in range(width - 2):
            hist_ref[k] = hist_ref[k + 1]
        hist_ref[width - 2] = cur
        return out

    def scan(a, u):
        h = a * h_ref[...] + u
        h_ref[...] = h
        return h

    new_x, _, _, _ = _mixer_core(
        x_ref[...], mod_ref, n1g_ref, win_ref, cb_ref, wa_ref, ba_ref, wx_ref, bx_ref, lam_ref,
        gl_ref, gs_ref, wout_ref,
        lru_conv=lambda cur: conv(cur, lb_ref, cw_ref),
        lru_scan=scan,
        sc_conv=lambda cur: conv(cur, sb_ref, sw_ref))
    x1_ref[...] = new_x


def _mixer_weight_specs(lw):
    return [_const_spec(lw[k].shape) for k in
            ("norm1_g", "w_in", "lru_conv_w", "lru_conv_b", "wa_bd", "lru_ba", "wx_bd", "lru_bx",
             "lru_lambda", "sconv_w", "gnorm_lru_g", "gnorm_sc_g", "w_out")]


def _mixer_weight_args(lw):
    return [lw[k] for k in
            ("norm1_g", "w_in", "lru_conv_w", "lru_conv_b", "wa_bd", "lru_ba", "wx_bd", "lru_bx",
             "lru_lambda", "sconv_w", "gnorm_lru_g", "gnorm_sc_g", "w_out")]


def _mixer_prompt(x_flat, mod_p, lw, batch, seq):
    t, d = x_flat.shape
    w = lw["lru_conv_b"].shape[-1]
    v = lw["gnorm_sc_g"].shape[-1]
    r = PROMPT_TILE
    tps = seq // r
    state = lambda c: pl.BlockSpec((None, SUBLANES, c), lambda i: (i // tps, 0, 0))
    return pl.pallas_call(
        functools.partial(_mixer_prompt_kernel, tiles_per_seq=tps),
        grid=(t // r,),
        in_specs=[pl.BlockSpec((r, d), lambda i: (i, 0)),
                  pl.BlockSpec((6, None, 1, d), lambda i: (0, i // tps, 0, 0))] + _mixer_weight_specs(lw),
        out_specs=[pl.BlockSpec((r, d), lambda i: (i, 0)), state(w), state(w), state(v)],
        out_shape=[jax.ShapeDtypeStruct((t, d), F32),
                   jax.ShapeDtypeStruct((batch, SUBLANES, w), F32),
                   jax.ShapeDtypeStruct((batch, SUBLANES, w), F32),
                   jax.ShapeDtypeStruct((batch, SUBLANES, v), F32)],
        compiler_params=_params(("arbitrary",), 60),
    )(x_flat, mod_p, *_mixer_weight_args(lw))


def _mixer_sample(x_flat, mod_s, lw, h0, lb0, sb0, batch):
    t, d = x_flat.shape
    w = lw["lru_conv_b"].shape[-1]
    v = lw["gnorm_sc_g"].shape[-1]
    full = lambda shape: pl.BlockSpec(shape, lambda i: (0,) * len(shape))
    return pl.pallas_call(
        _mixer_sample_kernel,
        grid=(t // batch,),
        in_specs=[pl.BlockSpec((batch, d), lambda i: (i, 0)), _const_spec(mod_s.shape)]
        + _mixer_weight_specs(lw) + [_const_spec(h0.shape), _const_spec(lb0.shape), _const_spec(sb0.shape)],
        out_specs=[pl.BlockSpec((batch, d), lambda i: (i, 0)), full(h0.shape), full(lb0.shape), full(sb0.shape)],
        out_shape=[jax.ShapeDtypeStruct((t, d), F32),
                   jax.ShapeDtypeStruct(h0.shape, F32),
                   jax.ShapeDtypeStruct(lb0.shape, F32),
                   jax.ShapeDtypeStruct(sb0.shape, F32)],
        compiler_params=_params(("arbitrary",), 60),
    )(x_flat, mod_s, *_mixer_weight_args(lw), h0, lb0, sb0)


def _top_values(work, k):
    kp = -(-k // SUBLANES) * SUBLANES
    row = lax.broadcasted_iota(jnp.int32, (kp, 1), 0)
    out = jnp.full((kp, work.shape[1]), -jnp.inf, F32)
    for j in range(k):
        m = jnp.max(work, axis=0, keepdims=True)
        out = jnp.where(row == j, m, out)
        if j + 1 < k:
            work = jnp.where(work == m, -jnp.inf, work)
    return out


def _prep_kernel(x1_ref, mod_ref, n2g_ref, wq_ref, sk_ref, h2t_ref, th_ref, e1_ref, s2_ref, e2_ref):
    nk = sk_ref.shape[1]
    n_heads = wq_ref.shape[1] // (2 * nk)
    k = PEER_TOPK
    shift2, scale2 = mod_ref[3], mod_ref[4]
    h2 = _rms(x1_ref[...], n2g_ref[...]) * (1.0 + scale2) + shift2
    h2t_ref[...] = h2.T.astype(BF16)
    qb = _dot(h2.astype(BF16), wq_ref[...]).astype(BF16)
    nt = (((1,), (1,)), ((), ()))
    for hd in range(n_heads):
        s = []
        for p in range(2):
            c = 2 * hd + p
            s.append(lax.dot_general(sk_ref[p], qb[:, c * nk:(c + 1) * nk], nt, preferred_element_type=F32))
        v1 = _top_values(s[0], k + 1)
        v2 = _top_values(s[1], k + 1)
        blocks = [v1[0:1, :] + v2]
        for a in range(1, SUBLANES):
            blocks.append(v1[a:a + 1, :] + v2[0:SUBLANES, :])
        blocks.append(v1[SUBLANES:, :] + v2[0:1, :])
        top = _top_values(jnp.concatenate(blocks, axis=0), k + 1)
        m = top[0:1, :]
        row = lax.broadcasted_iota(jnp.int32, (top.shape[0], 1), 0)
        z = jnp.sum(jnp.where(row < k, jnp.exp(top - m), 0.0), axis=0, keepdims=True)
        tau = 0.5 * (top[k - 1:k, :] + top[k:k + 1, :])
        th_ref[hd] = tau - s[0]
        e1_ref[hd] = jnp.exp(s[0] - v1[0:1, :]) * (1.0 / z)
        s2_ref[hd * nk:(hd + 1) * nk, :] = s[1]
        e2_ref[hd * nk:(hd + 1) * nk, :] = jnp.exp(s[1] - v2[0:1, :])


def _peer_prep(x1_flat, mod, mod_spec, rows, lw):
    t, d = x1_flat.shape
    nq = lw["peer_wq"].shape[1]
    nk = lw["peer_sub_keys"].shape[1]
    n_heads = nq // (2 * nk)
    return pl.pallas_call(
        _prep_kernel,
        grid=(t // rows,),
        in_specs=[pl.BlockSpec((rows, d), lambda i: (i, 0)), mod_spec,
                  _const_spec(lw["norm2_g"].shape), _const_spec(lw["peer_wq"].shape),
                  _const_spec(lw["peer_sub_keys"].shape)],
        out_specs=[pl.BlockSpec((d, rows), lambda i: (0, i)),
                   pl.BlockSpec((n_heads, nk, rows), lambda i: (0, 0, i)),
                   pl.BlockSpec((n_heads, nk, rows), lambda i: (0, 0, i)),
                   pl.BlockSpec((n_heads * nk, rows), lambda i: (0, i)),
                   pl.BlockSpec((n_heads * nk, rows), lambda i: (0, i))],
        out_shape=[jax.ShapeDtypeStruct((d, t), BF16),
                   jax.ShapeDtypeStruct((n_heads, nk, t), F32),
                   jax.ShapeDtypeStruct((n_heads, nk, t), F32),
                   jax.ShapeDtypeStruct((n_heads * nk, t), F32),
                   jax.ShapeDtypeStruct((n_heads * nk, t), F32)],
        compiler_params=_params(("arbitrary",), 48),
    )(x1_flat, mod, lw["norm2_g"], lw["peer_wq"], lw["peer_sub_keys"])


def _peer_kernel(h2t_ref, th_ref, e1_ref, s2_ref, e2_ref, u_ref, vt_ref, out_ref,
                 bc_ref, s0_ref, s1_ref, a0_ref, a1_ref, *, n_heads, nk, n_chunks):
    g = pl.program_id(0)
    c_out = jnp.maximum(g - 2, 0) % n_chunks
    ec, tb = s0_ref.shape
    d = out_ref.shape[0]
    i_group = bc_ref.shape[0]
    n_iter = ec // (i_group * nk)
    s_rows = i_group * nk
    o_rows = d // n_iter
    jrows = 2 * SUBLANES
    gl = min(PEER_GATE_LANES, tb)

    @pl.when(g == 0)
    def _():
        s1_ref[...] = jnp.zeros_like(s1_ref)
        a0_ref[...] = jnp.zeros_like(a0_ref)
        a1_ref[...] = jnp.zeros_like(a1_ref)

    @pl.when(c_out == 0)
    def _():
        out_ref[...] = jnp.zeros_like(out_ref)

    def run(s_w, s_r, a_w, a_r):
        def score_piece(r0, rows):
            s_w[r0:r0 + rows, :] = _dot(u_ref[r0:r0 + rows, :], h2t_ref[...])

        def out_piece(r0, rows):
            out_ref[r0:r0 + rows, :] += _dot(vt_ref[r0:r0 + rows, :], a_r[...])

        def gate_piece(r1, jp, l0):
            gates = [[jnp.zeros((SUBLANES, gl), F32) for _ in range(2)] for _ in range(i_group)]
            for hd in range(n_heads):
                thb = [bc_ref[ii, 2 * hd, :, l0:l0 + gl] for ii in range(i_group)]
                e1b = [bc_ref[ii, 2 * hd + 1, :, l0:l0 + gl] for ii in range(i_group)]
                for jh in range(2):
                    j0 = hd * nk + jp * jrows + jh * SUBLANES
                    s2 = s2_ref[j0:j0 + SUBLANES, l0:l0 + gl]
                    e2 = e2_ref[j0:j0 + SUBLANES, l0:l0 + gl]
                    for ii in range(i_group):
                        gates[ii][jh] = gates[ii][jh] + jnp.where(s2 >= thb[ii], e2 * e1b[ii], 0.0)
            for ii in range(i_group):
                row = r1 + ii * nk + jp * jrows
                act = jax.nn.gelu(s_r[row:row + jrows, l0:l0 + gl]) * jnp.concatenate(gates[ii], axis=0)
                a_w[row:row + jrows, l0:l0 + gl] = act.astype(BF16)

        for it in range(n_iter):
            r1 = it * s_rows
            r3 = it * o_rows
            for ii in range(i_group):
                il = it * i_group + ii
                for hd in range(n_heads):
                    bc_ref[ii, 2 * hd] = jnp.broadcast_to(th_ref[hd, il:il + 1, :], (SUBLANES, tb))
                    bc_ref[ii, 2 * hd + 1] = jnp.broadcast_to(e1_ref[hd, il:il + 1, :], (SUBLANES, tb))
            mxu = [functools.partial(score_piece, r1 + h * (s_rows // 2), s_rows // 2) for h in range(2)]
            mxu += [functools.partial(out_piece, r3 + h * (o_rows // 2), o_rows // 2) for h in range(2)]
            mxu = [mxu[0], mxu[2], mxu[1], mxu[3]]
            vpu = [functools.partial(gate_piece, r1, jp, l0)
                   for jp, l0 in itertools.product(range(nk // jrows), range(0, tb, gl))]
            per = len(vpu) // len(mxu)
            for k, m_piece in enumerate(mxu):
                m_piece()
                for v_piece in vpu[k * per:(k + 1) * per]:
                    v_piece()

    @pl.when(g % 2 == 0)
    def _():
        run(s0_ref, s1_ref, a1_ref, a0_ref)

    @pl.when(g % 2 == 1)
    def _():
        run(s1_ref, s0_ref, a0_ref, a1_ref)


def _peer_experts(h2t, th, e1, s2, e2, u_bf, vt_bf):
    d, t = h2t.shape
    n_heads, nk, _ = th.shape
    ne = u_bf.shape[0]
    tb = min(PEER_TOKENS, t)
    ec = PEER_EXPERTS
    nc = ne // ec
    n = (t // tb) * nc
    gate_step = lambda g: jnp.clip(g - 1, 0, n - 1)
    chunk_tab = pl.BlockSpec((n_heads, ec // nk, tb), lambda g: (0, gate_step(g) % nc, gate_step(g) // nc))
    token_tab = pl.BlockSpec((n_heads * nk, tb), lambda g: (0, gate_step(g) // nc),
                             pipeline_mode=pl.Buffered(1))
    return pl.pallas_call(
        functools.partial(_peer_kernel, n_heads=n_heads, nk=nk, n_chunks=nc),
        grid=(n + 2,),
        in_specs=[pl.BlockSpec((d, tb), lambda g: (0, jnp.minimum(g, n - 1) // nc),
                               pipeline_mode=pl.Buffered(1)),
                  chunk_tab, chunk_tab, token_tab, token_tab,
                  pl.BlockSpec((ec, d), lambda g: (jnp.minimum(g, n - 1) % nc, 0)),
                  pl.BlockSpec((d, ec), lambda g: (0, jnp.maximum(g - 2, 0) % nc))],
        out_specs=pl.BlockSpec((d, tb), lambda g: (0, jnp.maximum(g - 2, 0) // nc)),
        out_shape=jax.ShapeDtypeStruct((d, t), F32),
        scratch_shapes=[pltpu.VMEM((PEER_I_GROUP, 2 * n_heads, SUBLANES, tb), F32),
                        pltpu.VMEM((ec, tb), F32),
                        pltpu.VMEM((ec, tb), F32),
                        pltpu.VMEM((ec, tb), BF16),
                        pltpu.VMEM((ec, tb), BF16)],
        compiler_params=_params(("arbitrary",), 52),
    )(h2t, th, e1, s2, e2, u_bf, vt_bf)


def _epilogue_kernel(pt_ref, x1_ref, mod_ref, fg_ref, y_ref, *, final_norm):
    xo = x1_ref[...] + mod_ref[5] * pt_ref[...].T
    y_ref[...] = _rms(xo, fg_ref[...]) if final_norm else xo


def _epilogue(peer_t, x1_flat, mod, mod_spec, rows, final_g, final_norm):
    t, d = x1_flat.shape
    return pl.pallas_call(
        functools.partial(_epilogue_kernel, final_norm=final_norm),
        grid=(t // rows,),
        in_specs=[pl.BlockSpec((d, rows), lambda i: (0, i)),
                  pl.BlockSpec((rows, d), lambda i: (i, 0)), mod_spec, _const_spec(final_g.shape)],
        out_specs=pl.BlockSpec((rows, d), lambda i: (i, 0)),
        out_shape=jax.ShapeDtypeStruct((t, d), F32),
        compiler_params=_params(("arbitrary",), 40),
    )(peer_t, x1_flat, mod, final_g)


def _block_diag(w):
    heads, hd, _ = w.shape
    per = MXU_DIM // hd
    wg = w.reshape(heads // per, per, hd, hd)
    eye = jnp.eye(per, dtype=w.dtype)
    return jnp.einsum("gpij,pq->gpiqj", wg, eye).reshape(heads // per, MXU_DIM, MXU_DIM).astype(BF16)


def kernel(x_prompt, x_sample, c_prompt, c_sample, state_lru_h, state_lru_conv, state_sconv, w_ada, b_ada, norm1_g, norm2_g, w_in, lru_conv_w, lru_conv_b, lru_wa, lru_ba, lru_wx, lru_bx, lru_lambda, sconv_w, gnorm_lru_g, gnorm_sc_g, w_out, peer_wq, peer_sub_keys, peer_u, peer_v, final_g):
    bp, seq, d = x_prompt.shape
    bs, sseq, _ = x_sample.shape
    depth = w_ada.shape[0]
    nk = peer_sub_keys.shape[2]
    n_heads = peer_wq.shape[2] // (2 * nk)
    tps = seq // PROMPT_TILE

    xp = x_prompt.reshape(bp * seq, d)
    xs = jnp.swapaxes(x_sample, 0, 1).reshape(sseq * bs, d)
    n_c = bp + bs
    pad = (-n_c) % SUBLANES
    c_all = jnp.concatenate([c_prompt, c_sample, jnp.zeros((pad, d), F32)], axis=0)
    fg = final_g.reshape(1, d)

    mod_p_spec = pl.BlockSpec((6, None, 1, d), lambda i: (0, i // tps, 0, 0))
    mod_s_spec = _const_spec((6, bs, d))

    outs = [[] for _ in range(6)]
    for l in range(depth):
        row = lambda a: a[l].reshape(1, -1)
        lw = dict(
            norm1_g=row(norm1_g), norm2_g=row(norm2_g), w_in=w_in[l].astype(BF16),
            lru_conv_w=lru_conv_w[l], lru_conv_b=row(lru_conv_b),
            wa_bd=_block_diag(lru_wa[l]), lru_ba=row(lru_ba),
            wx_bd=_block_diag(lru_wx[l]), lru_bx=row(lru_bx), lru_lambda=row(lru_lambda),
            sconv_w=sconv_w[l], gnorm_lru_g=row(gnorm_lru_g), gnorm_sc_g=row(gnorm_sc_g),
            w_out=w_out[l].astype(BF16), peer_wq=peer_wq[l].astype(BF16),
            peer_sub_keys=peer_sub_keys[l].astype(BF16))
        u_bf = peer_u[l].astype(BF16)
        vt_bf = peer_v[l].T.astype(BF16)

        mod = _adaln(c_all, w_ada[l], b_ada[l])
        mod_p = jnp.swapaxes(mod[:bp].reshape(bp, 6, d), 0, 1).reshape(6, bp, 1, d)
        mod_s = jnp.swapaxes(mod[bp:n_c].reshape(bs, 6, d), 0, 1)

        xp1, hl, lb, sb = _mixer_prompt(xp, mod_p, lw, bp, seq)
        h0 = state_lru_h[l]
        lb0 = jnp.swapaxes(state_lru_conv[l], 0, 1)
        sb0 = jnp.swapaxes(state_sconv[l], 0, 1)
        xs1, hs, lbs, sbs = _mixer_sample(xs, mod_s, lw, h0, lb0, sb0, bs)

        last = l == depth - 1
        new_x = []
        for x1, mod_g, spec, rows in ((xp1, mod_p, mod_p_spec, PROMPT_TILE), (xs1, mod_s, mod_s_spec, bs)):
            peer_t = _peer_experts(*_peer_prep(x1, mod_g, spec, rows, lw), u_bf, vt_bf)
            new_x.append(_epilogue(peer_t, x1, mod_g, spec, rows, fg, last))
        xp, xs = new_x

        nl = lru_conv_w.shape[1] - 1
        ns = sconv_w.shape[1] - 1
        outs[0].append(hl[:, SUBLANES - 1, :])
        outs[1].append(lb[:, SUBLANES - nl:, :])
        outs[2].append(sb[:, SUBLANES - ns:, :])
        outs[3].append(hs)
        outs[4].append(jnp.swapaxes(lbs, 0, 1))
        outs[5].append(jnp.swapaxes(sbs, 0, 1))

    y_prompt = xp.reshape(bp, seq, d)
    y_sample = jnp.swapaxes(xs.reshape(sseq, bs, d), 0, 1)
    return (y_prompt, y_sample) + tuple(jnp.stack(o) for o in outs)
```

```python
import functools
import itertools

import jax
import jax.numpy as jnp
from jax import lax
from jax.experimental import pallas as pl
from jax.experimental.pallas import tpu as pltpu

F32 = jnp.float32
BF16 = jnp.bfloat16

EPS = 1e-6
LRU_C = 8.0
PEER_TOPK = 16

LANES = 128
SUBLANES = 8
MXU_DIM = 256
VMEM_BYTES = 64 * 1024 * 1024

PROMPT_TILE = 256
PEER_TOKENS = 512
PEER_EXPERTS = 1024
PEER_I_GROUP = 2
PEER_GATE_LANES = 256
PEER_J_TILES = 2
ADALN_COLS = 1024


def _dot(a, b):
    return jnp.dot(a, b, preferred_element_type=F32)


def _rms(x, g):
    ms = jnp.mean(x * x, axis=-1, keepdims=True)
    return x * lax.rsqrt(ms + EPS) * g


def _const_spec(shape):
    nd = len(shape)
    return pl.BlockSpec(shape, lambda *_: (0,) * nd, pipeline_mode=pl.Buffered(1))


def _params(sem, vmem_mb):
    return pltpu.CompilerParams(dimension_semantics=sem, vmem_limit_bytes=vmem_mb * 1024 * 1024)


def _adaln_kernel(c_ref, w_ref, b_ref, o_ref):
    sc = jax.nn.silu(c_ref[...]).astype(BF16)
    o_ref[...] = _dot(sc, w_ref[...].astype(BF16)) + b_ref[...]


def _adaln(c_all, w_ada, b_ada):
    m, d = c_all.shape
    n = w_ada.shape[1]
    tn = ADALN_COLS
    return pl.pallas_call(
        _adaln_kernel,
        grid=(n // tn,),
        in_specs=[pl.BlockSpec((m, d), lambda j: (0, 0)),
                  pl.BlockSpec((d, tn), lambda j: (0, j)),
                  pl.BlockSpec((1, tn), lambda j: (0, j))],
        out_specs=pl.BlockSpec((m, tn), lambda j: (0, j)),
        out_shape=jax.ShapeDtypeStruct((m, n), F32),
        compiler_params=_params(("arbitrary",), 40),
    )(c_all, w_ada, b_ada.reshape(1, n))


def _gate_pre(xcb, w_ref, b_ref):
    nblk = w_ref.shape[0]
    parts = [_dot(xcb[:, g * MXU_DIM:(g + 1) * MXU_DIM], w_ref[g]) for g in range(nblk)]
    return jnp.concatenate(parts, axis=1) + b_ref[...]


def _scan_rows(a, u, h_in):
    rows = a.shape[0]
    row = lax.broadcasted_iota(jnp.int32, (rows, 1), 0)
    acc_a = a
    acc_b = u + jnp.where(row == 0, a * h_in, 0.0)
    d = 1
    while d < rows:
        keep = row >= d
        sh_b = pltpu.roll(acc_b, d, axis=0)
        acc_b = acc_b + jnp.where(keep, acc_a * sh_b, 0.0)
        if 2 * d < rows:
            sh_a = pltpu.roll(acc_a, d, axis=0)
            acc_a = jnp.where(keep, acc_a * sh_a, acc_a)
        d *= 2
    return acc_b


def _mixer_core(x, mod_ref, n1g_ref, win_ref, cb_ref, wa_ref, ba_ref, wx_ref, bx_ref, lam_ref,
                gl_ref, gs_ref, wout_ref, lru_conv, lru_scan, sc_conv):
    w = cb_ref.shape[-1]
    v = gs_ref.shape[-1]
    shift1, scale1, gate1 = mod_ref[0], mod_ref[1], mod_ref[2]
    h = _rms(x, n1g_ref[...]) * (1.0 + scale1) + shift1
    hb = h.astype(BF16)

    x_lru = _dot(hb, win_ref[:, 0:w])
    xc = lru_conv(x_lru) + cb_ref[...]
    xcb = xc.astype(BF16)
    r = jax.nn.sigmoid(_gate_pre(xcb, wa_ref, ba_ref))
    i = jax.nn.sigmoid(_gate_pre(xcb, wx_ref, bx_ref))
    log_a = (-LRU_C) * r * jax.nn.softplus(-lam_ref[...])
    a = jnp.exp(log_a)
    u = jnp.sqrt(jnp.tanh(-log_a) * (1.0 + a * a)) * (i * xc)
    rec = lru_scan(a, u)
    y_gate = _dot(hb, win_ref[:, w:2 * w])
    out_lru = rec * jax.nn.gelu(y_gate)

    sc_b = _dot(hb, win_ref[:, 2 * w:2 * w + v])
    sc_c = _dot(hb, win_ref[:, 2 * w + v:2 * w + 2 * v])
    sc_x = _dot(hb, win_ref[:, 2 * w + 2 * v:2 * w + 3 * v])
    cx = sc_c * sc_x
    out_sc = sc_b * sc_conv(cx)

    n_l = _rms(out_lru, gl_ref[...]).astype(BF16)
    n_s = _rms(out_sc, gs_ref[...]).astype(BF16)
    mix = _dot(n_l, wout_ref[0:w, :]) + _dot(n_s, wout_ref[w:w + v, :])
    return x + gate1 * mix, x_lru, rec, cx


def _mixer_prompt_kernel(x_ref, mod_ref, n1g_ref, win_ref, cw_ref, cb_ref, wa_ref, ba_ref, wx_ref,
                         bx_ref, lam_ref, sw_ref, gl_ref, gs_ref, wout_ref,
                         x1_ref, hl_ref, lb_ref, sb_ref, *, tiles_per_seq):
    rows = x_ref.shape[0]
    hdr = SUBLANES

    @pl.when(pl.program_id(0) % tiles_per_seq == 0)
    def _():
        hl_ref[...] = jnp.zeros_like(hl_ref)
        lb_ref[...] = jnp.zeros_like(lb_ref)
        sb_ref[...] = jnp.zeros_like(sb_ref)

    def conv(cur, tail_ref, w_ref):
        width = w_ref.shape[0]
        ext = jnp.concatenate([tail_ref[...], cur], axis=0)
        out = w_ref[width - 1:width, :] * cur
        for d in range(1, width):
            out = out + w_ref[width - 1 - d:width - d, :] * pltpu.roll(ext, d, axis=0)[hdr:]
        return out

    new_x, x_lru, rec, cx = _mixer_core(
        x_ref[...], mod_ref, n1g_ref, win_ref, cb_ref, wa_ref, ba_ref, wx_ref, bx_ref, lam_ref,
        gl_ref, gs_ref, wout_ref,
        lru_conv=lambda cur: conv(cur, lb_ref, cw_ref),
        lru_scan=lambda a, u: _scan_rows(a, u, hl_ref[hdr - 1:hdr, :]),
        sc_conv=lambda cur: conv(cur, sb_ref, sw_ref))
    x1_ref[...] = new_x
    hl_ref[...] = rec[rows - hdr:, :]
    lb_ref[...] = x_lru[rows - hdr:, :]
    sb_ref[...] = cx[rows - hdr:, :]


def _mixer_sample_kernel(x_ref, mod_ref, n1g_ref, win_ref, cw_ref, cb_ref, wa_ref, ba_ref, wx_ref,
                         bx_ref, lam_ref, sw_ref, gl_ref, gs_ref, wout_ref, h0_ref, lb0_ref, sb0_ref,
                         x1_ref, h_ref, lb_ref, sb_ref):
    @pl.when(pl.program_id(0) == 0)
    def _():
        h_ref[...] = h0_ref[...]
        lb_ref[...] = lb0_ref[...]
        sb_ref[...] = sb0_ref[...]

    def conv(cur, hist_ref, w_ref):
        width = w_ref.shape[0]
        out = w_ref[width - 1:width, :] * cur
        for k in range(width - 1):
            out = out + w_ref[k:k + 1, :] * hist_ref[k]
        for k in range(width - 2):
            hist_ref[k] = hist_ref[k + 1]
        hist_ref[width - 2] = cur
        return out

    def scan(a, u):
        h = a * h_ref[...] + u
        h_ref[...] = h
        return h

    new_x, _, _, _ = _mixer_core(
        x_ref[...], mod_ref, n1g_ref, win_ref, cb_ref, wa_ref, ba_ref, wx_ref, bx_ref, lam_ref,
        gl_ref, gs_ref, wout_ref,
        lru_conv=lambda cur: conv(cur, lb_ref, cw_ref),
        lru_scan=scan,
        sc_conv=lambda cur: conv(cur, sb_ref, sw_ref))
    x1_ref[...] = new_x


def _mixer_weight_specs(lw):
    return [_const_spec(lw[k].shape) for k in
            ("norm1_g", "w_in", "lru_conv_w", "lru_conv_b", "wa_bd", "lru_ba", "wx_bd", "lru_bx",
             "lru_lambda", "sconv_w", "gnorm_lru_g", "gnorm_sc_g", "w_out")]


def _mixer_weight_args(lw):
    return [lw[k] for k in
            ("norm1_g", "w_in", "lru_conv_w", "lru_conv_b", "wa_bd", "lru_ba", "wx_bd", "lru_bx",
             "lru_lambda", "sconv_w", "gnorm_lru_g", "gnorm_sc_g", "w_out")]


def _mixer_prompt(x_flat, mod_p, lw, batch, seq):
    t, d = x_flat.shape
    w = lw["lru_conv_b"].shape[-1]
    v = lw["gnorm_sc_g"].shape[-1]
    r = PROMPT_TILE
    tps = seq // r
    state = lambda c: pl.BlockSpec((None, SUBLANES, c), lambda i: (i // tps, 0, 0))
    return pl.pallas_call(
        functools.partial(_mixer_prompt_kernel, tiles_per_seq=tps),
        grid=(t // r,),
        in_specs=[pl.BlockSpec((r, d), lambda i: (i, 0)),
                  pl.BlockSpec((6, None, 1, d), lambda i: (0, i // tps, 0, 0))] + _mixer_weight_specs(lw),
        out_specs=[pl.BlockSpec((r, d), lambda i: (i, 0)), state(w), state(w), state(v)],
        out_shape=[jax.ShapeDtypeStruct((t, d), F32),
                   jax.ShapeDtypeStruct((batch, SUBLANES, w), F32),
                   jax.ShapeDtypeStruct((batch, SUBLANES, w), F32),
                   jax.ShapeDtypeStruct((batch, SUBLANES, v), F32)],
        compiler_params=_params(("arbitrary",), 60),
    )(x_flat, mod_p, *_mixer_weight_args(lw))


def _mixer_sample(x_flat, mod_s, lw, h0, lb0, sb0, batch):
    t, d = x_flat.shape
    w = lw["lru_conv_b"].shape[-1]
    v = lw["gnorm_sc_g"].shape[-1]
    full = lambda shape: pl.BlockSpec(shape, lambda i: (0,) * len(shape))
    return pl.pallas_call(
        _mixer_sample_kernel,
        grid=(t // batch,),
        in_specs=[pl.BlockSpec((batch, d), lambda i: (i, 0)), _const_spec(mod_s.shape)]
        + _mixer_weight_specs(lw) + [_const_spec(h0.shape), _const_spec(lb0.shape), _const_spec(sb0.shape)],
        out_specs=[pl.BlockSpec((batch, d), lambda i: (i, 0)), full(h0.shape), full(lb0.shape), full(sb0.shape)],
        out_shape=[jax.ShapeDtypeStruct((t, d), F32),
                   jax.ShapeDtypeStruct(h0.shape, F32),
                   jax.ShapeDtypeStruct(lb0.shape, F32),
                   jax.ShapeDtypeStruct(sb0.shape, F32)],
        compiler_params=_params(("arbitrary",), 60),
    )(x_flat, mod_s, *_mixer_weight_args(lw), h0, lb0, sb0)


def _top_values(work, k, with_rank=False):
    kp = -(-k // SUBLANES) * SUBLANES
    row = lax.broadcasted_iota(jnp.int32, (kp, 1), 0)
    out = jnp.full((kp, work.shape[1]), -jnp.inf, F32)
    rank = jnp.full(work.shape, float(work.shape[0]), F32) if with_rank else None
    for j in range(k):
        m = jnp.max(work, axis=0, keepdims=True)
        out = jnp.where(row == j, m, out)
        hit = work == m
        if with_rank:
            rank = jnp.where(hit, float(j), rank)
        if j + 1 < k:
            work = jnp.where(hit, -jnp.inf, work)
    return (out, rank) if with_rank else out


def _prep_kernel(x1_ref, mod_ref, n2g_ref, wq_ref, sk_ref, h2t_ref, kc_ref, e1_ref, r2_ref, e2_ref):
    nk = sk_ref.shape[1]
    n_heads = wq_ref.shape[1] // (2 * nk)
    k = PEER_TOPK
    shift2, scale2 = mod_ref[3], mod_ref[4]
    h2 = _rms(x1_ref[...], n2g_ref[...]) * (1.0 + scale2) + shift2
    h2t_ref[...] = h2.T.astype(BF16)
    qb = _dot(h2.astype(BF16), wq_ref[...]).astype(BF16)
    nt = (((1,), (1,)), ((), ()))
    for hd in range(n_heads):
        s1, s2 = [lax.dot_general(sk_ref[p], qb[:, (2 * hd + p) * nk:(2 * hd + p + 1) * nk], nt,
                                  preferred_element_type=F32) for p in range(2)]
        v1 = _top_values(s1, k)
        v2, rank2 = _top_values(s2, k, with_rank=True)
        blocks = [v1[0:1, :] + v2]
        for a in range(1, SUBLANES):
            blocks.append(v1[a:a + 1, :] + v2[0:SUBLANES, :])
        blocks.append(v1[SUBLANES:, :] + v2[0:1, :])
        top = _top_values(jnp.concatenate(blocks, axis=0), k)
        z = jnp.sum(jnp.exp(top - top[0:1, :]), axis=0, keepdims=True)
        tau = top[k - 1:k, :]
        count = jnp.zeros_like(s1)
        for b in range(k):
            count = count + jnp.where(s1 + v2[b:b + 1, :] >= tau, 1.0, 0.0)
        kc_ref[hd] = count
        e1_ref[hd] = jnp.exp(s1 - v1[0:1, :]) * (1.0 / z)
        r2_ref[hd * nk:(hd + 1) * nk, :] = rank2.astype(BF16)
        e2_ref[hd * nk:(hd + 1) * nk, :] = jnp.exp(s2 - v2[0:1, :]).astype(BF16)


def _peer_prep(x1_flat, mod, mod_spec, rows, lw):
    t, d = x1_flat.shape
    nq = lw["peer_wq"].shape[1]
    nk = lw["peer_sub_keys"].shape[1]
    n_heads = nq // (2 * nk)
    return pl.pallas_call(
        _prep_kernel,
        grid=(t // rows,),
        in_specs=[pl.BlockSpec((rows, d), lambda i: (i, 0)), mod_spec,
                  _const_spec(lw["norm2_g"].shape), _const_spec(lw["peer_wq"].shape),
                  _const_spec(lw["peer_sub_keys"].shape)],
        out_specs=[pl.BlockSpec((d, rows), lambda i: (0, i)),
                   pl.BlockSpec((n_heads, nk, rows), lambda i: (0, 0, i)),
                   pl.BlockSpec((n_heads, nk, rows), lambda i: (0, 0, i)),
                   pl.BlockSpec((n_heads * nk, rows), lambda i: (0, i)),
                   pl.BlockSpec((n_heads * nk, rows), lambda i: (0, i))],
        out_shape=[jax.ShapeDtypeStruct((d, t), BF16),
                   jax.ShapeDtypeStruct((n_heads, nk, t), F32),
                   jax.ShapeDtypeStruct((n_heads, nk, t), F32),
                   jax.ShapeDtypeStruct((n_heads * nk, t), BF16),
                   jax.ShapeDtypeStruct((n_heads * nk, t), BF16)],
        compiler_params=_params(("arbitrary",), 48),
    )(x1_flat, mod, lw["norm2_g"], lw["peer_wq"], lw["peer_sub_keys"])


def _peer_kernel(h2t_ref, kc_ref, e1_ref, r2_ref, e2_ref, u_ref, vt_ref, out_ref,
                 bc_ref, s0_ref, s1_ref, a_ref, *, n_heads, nk, n_chunks):
    g = pl.program_id(0)
    c_gate = jnp.maximum(g - 1, 0) % n_chunks
    ec, tb = s0_ref.shape
    i_group = bc_ref.shape[0]
    s_rows = i_group * nk
    jrows = bc_ref.shape[2]
    gl = min(PEER_GATE_LANES, tb)

    @pl.when(g == 0)
    def _():
        s1_ref[...] = jnp.zeros_like(s1_ref)

    @pl.when(c_gate == 0)
    def _():
        out_ref[...] = jnp.zeros_like(out_ref)

    def run(s_w, s_r):
        def gate_piece(r1, jq, l0):
            tiles = [jq * PEER_J_TILES + jt for jt in range(PEER_J_TILES)]
            gates = [[jnp.zeros((jrows, gl), BF16) for _ in tiles] for _ in range(i_group)]
            for hd in range(n_heads):
                kcb = [bc_ref[ii, 2 * hd, :, l0:l0 + gl] for ii in range(i_group)]
                e1b = [bc_ref[ii, 2 * hd + 1, :, l0:l0 + gl] for ii in range(i_group)]
                for n, jt in enumerate(tiles):
                    j0 = hd * nk + jt * jrows
                    r2 = r2_ref[j0:j0 + jrows, l0:l0 + gl]
                    e2 = e2_ref[j0:j0 + jrows, l0:l0 + gl]
                    for ii in range(i_group):
                        gates[ii][n] = gates[ii][n] + jnp.where(r2 < kcb[ii], e2 * e1b[ii], 0.0)
            for ii in range(i_group):
                for n, jt in enumerate(tiles):
                    row = r1 + ii * nk + jt * jrows
                    act = jax.nn.gelu(s_r[row:row + jrows, l0:l0 + gl]).astype(BF16) * gates[ii][n]
                    a_ref[row:row + jrows, l0:l0 + gl] = act

        s_w[...] = _dot(u_ref[...], h2t_ref[...])
        for it in range(ec // s_rows):
            r1 = it * s_rows
            for ii in range(i_group):
                il = it * i_group + ii
                for hd in range(n_heads):
                    bc_ref[ii, 2 * hd] = jnp.broadcast_to(kc_ref[hd, il:il + 1, :].astype(BF16), (jrows, tb))
                    bc_ref[ii, 2 * hd + 1] = jnp.broadcast_to(e1_ref[hd, il:il + 1, :].astype(BF16), (jrows, tb))
            for jq, l0 in itertools.product(range(nk // (jrows * PEER_J_TILES)), range(0, tb, gl)):
                gate_piece(r1, jq, l0)
            out_ref[...] += _dot(vt_ref[:, r1:r1 + s_rows], a_ref[r1:r1 + s_rows, :])

    @pl.when(g % 2 == 0)
    def _():
        run(s0_ref, s1_ref)

    @pl.when(g % 2 == 1)
    def _():
        run(s1_ref, s0_ref)


def _peer_experts(h2t, kc, e1, r2, e2, u_bf, vt_bf):
    d, t = h2t.shape
    n_heads, nk, _ = kc.shape
    ne = u_bf.shape[0]
    tb = min(PEER_TOKENS, t)
    ec = PEER_EXPERTS
    nc = ne // ec
    n = (t // tb) * nc
    gate_step = lambda g: jnp.maximum(g - 1, 0)
    chunk_tab = pl.BlockSpec((n_heads, ec // nk, tb), lambda g: (0, gate_step(g) % nc, gate_step(g) // nc))
    token_tab = pl.BlockSpec((n_heads * nk, tb), lambda g: (0, gate_step(g) // nc),
                             pipeline_mode=pl.Buffered(1))
    return pl.pallas_call(
        functools.partial(_peer_kernel, n_heads=n_heads, nk=nk, n_chunks=nc),
        grid=(n + 1,),
        in_specs=[pl.BlockSpec((d, tb), lambda g: (0, jnp.minimum(g, n - 1) // nc),
                               pipeline_mode=pl.Buffered(1)),
                  chunk_tab, chunk_tab, token_tab, token_tab,
                  pl.BlockSpec((ec, d), lambda g: (jnp.minimum(g, n - 1) % nc, 0)),
                  pl.BlockSpec((d, ec), lambda g: (0, gate_step(g) % nc))],
        out_specs=pl.BlockSpec((d, tb), lambda g: (0, gate_step(g) // nc)),
        out_shape=jax.ShapeDtypeStruct((d, t), F32),
        scratch_shapes=[pltpu.VMEM((PEER_I_GROUP, 2 * n_heads, 2 * SUBLANES, tb), BF16),
                        pltpu.VMEM((ec, tb), F32),
                        pltpu.VMEM((ec, tb), F32),
                        pltpu.VMEM((ec, tb), BF16)],
        compiler_params=_params(("arbitrary",), 52),
    )(h2t, kc, e1, r2, e2, u_bf, vt_bf)


def _epilogue_kernel(pt_ref, x1_ref, mod_ref, fg_ref, y_ref, *, final_norm):
    xo = x1_ref[...] + mod_ref[5] * pt_ref[...].T
    y_ref[...] = _rms(xo, fg_ref[...]) if final_norm else xo


def _epilogue(peer_t, x1_flat, mod, mod_spec, rows, final_g, final_norm):
    t, d = x1_flat.shape
    return pl.pallas_call(
        functools.partial(_epilogue_kernel, final_norm=final_norm),
        grid=(t // rows,),
        in_specs=[pl.BlockSpec((d, rows), lambda i: (0, i)),
                  pl.BlockSpec((rows, d), lambda i: (i, 0)), mod_spec, _const_spec(final_g.shape)],
        out_specs=pl.BlockSpec((rows, d), lambda i: (i, 0)),
        out_shape=jax.ShapeDtypeStruct((t, d), F32),
        compiler_params=_params(("arbitrary",), 40),
    )(peer_t, x1_flat, mod, final_g)


def _block_diag(w):
    heads, hd, _ = w.shape
    per = MXU_DIM // hd
    wg = w.reshape(heads // per, per, hd, hd)
    eye = jnp.eye(per, dtype=w.dtype)
    return jnp.einsum("gpij,pq->gpiqj", wg, eye).reshape(heads // per, MXU_DIM, MXU_DIM).astype(BF16)


def kernel(x_prompt, x_sample, c_prompt, c_sample, state_lru_h, state_lru_conv, state_sconv, w_ada, b_ada, norm1_g, norm2_g, w_in, lru_conv_w, lru_conv_b, lru_wa, lru_ba, lru_wx, lru_bx, lru_lambda, sconv_w, gnorm_lru_g, gnorm_sc_g, w_out, peer_wq, peer_sub_keys, peer_u, peer_v, final_g):
    bp, seq, d = x_prompt.shape
    bs, sseq, _ = x_sample.shape
    depth = w_ada.shape[0]
    nk = peer_sub_keys.shape[2]
    n_heads = peer_wq.shape[2] // (2 * nk)
    tps = seq // PROMPT_TILE

    xp = x_prompt.reshape(bp * seq, d)
    xs = jnp.swapaxes(x_sample, 0, 1).reshape(sseq * bs, d)
    n_c = bp + bs
    pad = (-n_c) % SUBLANES
    c_all = jnp.concatenate([c_prompt, c_sample, jnp.zeros((pad, d), F32)], axis=0)
    fg = final_g.reshape(1, d)

    mod_p_spec = pl.BlockSpec((6, None, 1, d), lambda i: (0, i // tps, 0, 0))
    mod_s_spec = _const_spec((6, bs, d))

    outs = [[] for _ in range(6)]
    for l in range(depth):
        row = lambda a: a[l].reshape(1, -1)
        lw = dict(
            norm1_g=row(norm1_g), norm2_g=row(norm2_g), w_in=w_in[l].astype(BF16),
            lru_conv_w=lru_conv_w[l], lru_conv_b=row(lru_conv_b),
            wa_bd=_block_diag(lru_wa[l]), lru_ba=row(lru_ba),
            wx_bd=_block_diag(lru_wx[l]), lru_bx=row(lru_bx), lru_lambda=row(lru_lambda),
            sconv_w=sconv_w[l], gnorm_lru_g=row(gnorm_lru_g), gnorm_sc_g=row(gnorm_sc_g),
            w_out=w_out[l].astype(BF16), peer_wq=peer_wq[l].astype(BF16),
            peer_sub_keys=peer_sub_keys[l].astype(BF16))
        u_bf = peer_u[l].astype(BF16)
        vt_bf = peer_v[l].T.astype(BF16)

        mod = _adaln(c_all, w_ada[l], b_ada[l])
        mod_p = jnp.swapaxes(mod[:bp].reshape(bp, 6, d), 0, 1).reshape(6, bp, 1, d)
        mod_s = jnp.swapaxes(mod[bp:n_c].reshape(bs, 6, d), 0, 1)

        xp1, hl, lb, sb = _mixer_prompt(xp, mod_p, lw, bp, seq)
        h0 = state_lru_h[l]
        lb0 = jnp.swapaxes(state_lru_conv[l], 0, 1)
        sb0 = jnp.swapaxes(state_sconv[l], 0, 1)
        xs1, hs, lbs, sbs = _mixer_sample(xs, mod_s, lw, h0, lb0, sb0, bs)

        last = l == depth - 1
        new_x = []
        for x1, mod_g, spec, rows in ((xp1, mod_p, mod_p_spec, PROMPT_TILE), (xs1, mod_s, mod_s_spec, bs)):
            peer_t = _peer_experts(*_peer_prep(x1, mod_g, spec, rows, lw), u_bf, vt_bf)
            new_x.append(_epilogue(peer_t, x1, mod_g, spec, rows, fg, last))
        xp, xs = new_x

        nl = lru_conv_w.shape[1] - 1
        ns = sconv_w.shape[1] - 1
        outs[0].append(hl[:, SUBLANES - 1, :])
        outs[1].append(lb[:, SUBLANES - nl:, :])
        outs[2].append(sb[:, SUBLANES - ns:, :])
        outs[3].append(hs)
        outs[4].append(jnp.swapaxes(lbs, 0, 1))
        outs[5].append(jnp.swapaxes(sbs, 0, 1))

    y_prompt = xp.reshape(bp, seq, d)
    y_sample = jnp.swapaxes(xs.reshape(sseq, bs, d), 0, 1)
    return (y_prompt, y_sample) + tuple(jnp.stack(o) for o in outs)
```

```python
import functools
import itertools

import jax
import jax.numpy as jnp
from jax import lax
from jax.experimental import pallas as pl
from jax.experimental.pallas import tpu as pltpu

F32 = jnp.float32
BF16 = jnp.bfloat16

EPS = 1e-6
LRU_C = 8.0
PEER_TOPK = 16

LANES = 128
SUBLANES = 8
MXU_DIM = 256
VMEM_BYTES = 64 * 1024 * 1024

PROMPT_TILE = 256
PEER_TOKENS = 512
PEER_EXPERTS = 1024
PEER_I_GROUP = 2
PEER_GATE_LANES = 256
PEER_J_TILES = 2
ADALN_COLS = 1024


def _dot(a, b):
    return jnp.dot(a, b, preferred_element_type=F32)


def _rms(x, g):
    ms = jnp.mean(x * x, axis=-1, keepdims=True)
    return x * lax.rsqrt(ms + EPS) * g


def _const_spec(shape):
    nd = len(shape)
    return pl.BlockSpec(shape, lambda *_: (0,) * nd, pipeline_mode=pl.Buffered(1))


def _params(sem, vmem_mb):
    return pltpu.CompilerParams(dimension_semantics=sem, vmem_limit_bytes=vmem_mb * 1024 * 1024)


def _adaln_kernel(c_ref, w_ref, b_ref, o_ref):
    sc = jax.nn.silu(c_ref[...]).astype(BF16)
    o_ref[...] = _dot(sc, w_ref[...].astype(BF16)) + b_ref[...]


def _adaln(c_all, w_ada, b_ada):
    m, d = c_all.shape
    n = w_ada.shape[1]
    tn = ADALN_COLS
    return pl.pallas_call(
        _adaln_kernel,
        grid=(n // tn,),
        in_specs=[pl.BlockSpec((m, d), lambda j: (0, 0)),
                  pl.BlockSpec((d, tn), lambda j: (0, j)),
                  pl.BlockSpec((1, tn), lambda j: (0, j))],
        out_specs=pl.BlockSpec((m, tn), lambda j: (0, j)),
        out_shape=jax.ShapeDtypeStruct((m, n), F32),
        compiler_params=_params(("arbitrary",), 40),
    )(c_all, w_ada, b_ada.reshape(1, n))


def _gate_pre(xcb, w_ref, b_ref):
    nblk = w_ref.shape[0]
    parts = [_dot(xcb[:, g * MXU_DIM:(g + 1) * MXU_DIM], w_ref[g]) for g in range(nblk)]
    return jnp.concatenate(parts, axis=1) + b_ref[...]


def _scan_rows(a, u, h_in):
    rows = a.shape[0]
    row = lax.broadcasted_iota(jnp.int32, (rows, 1), 0)
    acc_a = a
    acc_b = u + jnp.where(row == 0, a * h_in, 0.0)
    d = 1
    while d < rows:
        keep = row >= d
        sh_b = pltpu.roll(acc_b, d, axis=0)
        acc_b = acc_b + jnp.where(keep, acc_a * sh_b, 0.0)
        if 2 * d < rows:
            sh_a = pltpu.roll(acc_a, d, axis=0)
            acc_a = jnp.where(keep, acc_a * sh_a, acc_a)
        d *= 2
    return acc_b


def _mixer_core(x, mod_ref, n1g_ref, win_ref, cb_ref, wa_ref, ba_ref, wx_ref, bx_ref, lam_ref,
                gl_ref, gs_ref, wout_ref, lru_conv, lru_scan, sc_conv):
    w = cb_ref.shape[-1]
    v = gs_ref.shape[-1]
    shift1, scale1, gate1 = mod_ref[0], mod_ref[1], mod_ref[2]
    h = _rms(x, n1g_ref[...]) * (1.0 + scale1) + shift1
    hb = h.astype(BF16)

    x_lru = _dot(hb, win_ref[:, 0:w])
    xc = lru_conv(x_lru) + cb_ref[...]
    xcb = xc.astype(BF16)
    r = jax.nn.sigmoid(_gate_pre(xcb, wa_ref, ba_ref))
    i = jax.nn.sigmoid(_gate_pre(xcb, wx_ref, bx_ref))
    log_a = (-LRU_C) * r * jax.nn.softplus(-lam_ref[...])
    a = jnp.exp(log_a)
    u = jnp.sqrt(jnp.tanh(-log_a) * (1.0 + a * a)) * (i * xc)
    rec = lru_scan(a, u)
    y_gate = _dot(hb, win_ref[:, w:2 * w])
    out_lru = rec * jax.nn.gelu(y_gate)

    sc_b = _dot(hb, win_ref[:, 2 * w:2 * w + v])
    sc_c = _dot(hb, win_ref[:, 2 * w + v:2 * w + 2 * v])
    sc_x = _dot(hb, win_ref[:, 2 * w + 2 * v:2 * w + 3 * v])
    cx = sc_c * sc_x
    out_sc = sc_b * sc_conv(cx)

    n_l = _rms(out_lru, gl_ref[...]).astype(BF16)
    n_s = _rms(out_sc, gs_ref[...]).astype(BF16)
    mix = _dot(n_l, wout_ref[0:w, :]) + _dot(n_s, wout_ref[w:w + v, :])
    return x + gate1 * mix, x_lru, rec, cx


def _mixer_prompt_kernel(x_ref, mod_ref, n1g_ref, win_ref, cw_ref, cb_ref, wa_ref, ba_ref, wx_ref,
                         bx_ref, lam_ref, sw_ref, gl_ref, gs_ref, wout_ref,
                         x1_ref, hl_ref, lb_ref, sb_ref, *, tiles_per_seq):
    rows = x_ref.shape[0]
    hdr = SUBLANES

    @pl.when(pl.program_id(0) % tiles_per_seq == 0)
    def _():
        hl_ref[...] = jnp.zeros_like(hl_ref)
        lb_ref[...] = jnp.zeros_like(lb_ref)
        sb_ref[...] = jnp.zeros_like(sb_ref)

    def conv(cur, tail_ref, w_ref):
        width = w_ref.shape[0]
        ext = jnp.concatenate([tail_ref[...], cur], axis=0)
        out = w_ref[width - 1:width, :] * cur
        for d in range(1, width):
            out = out + w_ref[width - 1 - d:width - d, :] * pltpu.roll(ext, d, axis=0)[hdr:]
        return out

    new_x, x_lru, rec, cx = _mixer_core(
        x_ref[...], mod_ref, n1g_ref, win_ref, cb_ref, wa_ref, ba_ref, wx_ref, bx_ref, lam_ref,
        gl_ref, gs_ref, wout_ref,
        lru_conv=lambda cur: conv(cur, lb_ref, cw_ref),
        lru_scan=lambda a, u: _scan_rows(a, u, hl_ref[hdr - 1:hdr, :]),
        sc_conv=lambda cur: conv(cur, sb_ref, sw_ref))
    x1_ref[...] = new_x
    hl_ref[...] = rec[rows - hdr:, :]
    lb_ref[...] = x_lru[rows - hdr:, :]
    sb_ref[...] = cx[rows - hdr:, :]


def _mixer_sample_kernel(x_ref, mod_ref, n1g_ref, win_ref, cw_ref, cb_ref, wa_ref, ba_ref, wx_ref,
                         bx_ref, lam_ref, sw_ref, gl_ref, gs_ref, wout_ref, h0_ref, lb0_ref, sb0_ref,
                         x1_ref, h_ref, lb_ref, sb_ref):
    @pl.when(pl.program_id(0) == 0)
    def _():
        h_ref[...] = h0_ref[...]
        lb_ref[...] = lb0_ref[...]
        sb_ref[...] = sb0_ref[...]

    def conv(cur, hist_ref, w_ref):
        width = w_ref.shape[0]
        out = w_ref[width - 1:width, :] * cur
        for k in range(width - 1):
            out = out + w_ref[k:k + 1, :] * hist_ref[k]
        for k in range(width - 2):
            hist_ref[k] = hist_ref[k + 1]
        hist_ref[width - 2] = cur
        return out

    def scan(a, u):
        h = a * h_ref[...] + u
        h_ref[...] = h
        return h

    new_x, _, _, _ = _mixer_core(
        x_ref[...], mod_ref, n1g_ref, win_ref, cb_ref, wa_ref, ba_ref, wx_ref, bx_ref, lam_ref,
        gl_ref, gs_ref, wout_ref,
        lru_conv=lambda cur: conv(cur, lb_ref, cw_ref),
        lru_scan=scan,
        sc_conv=lambda cur: conv(cur, sb_ref, sw_ref))
    x1_ref[...] = new_x


def _mixer_weight_specs(lw):
    return [_const_spec(lw[k].shape) for k in
            ("norm1_g", "w_in", "lru_conv_w", "lru_conv_b", "wa_bd", "lru_ba", "wx_bd", "lru_bx",
             "lru_lambda", "sconv_w", "gnorm_lru_g", "gnorm_sc_g", "w_out")]


def _mixer_weight_args(lw):
    return [lw[k] for k in
            ("norm1_g", "w_in", "lru_conv_w", "lru_conv_b", "wa_bd", "lru_ba", "wx_bd", "lru_bx",
             "lru_lambda", "sconv_w", "gnorm_lru_g", "gnorm_sc_g", "w_out")]


def _mixer_prompt(x_flat, mod_p, lw, batch, seq):
    t, d = x_flat.shape
    w = lw["lru_conv_b"].shape[-1]
    v = lw["gnorm_sc_g"].shape[-1]
    r = PROMPT_TILE
    tps = seq // r
    state = lambda c: pl.BlockSpec((None, SUBLANES, c), lambda i: (i // tps, 0, 0))
    return pl.pallas_call(
        functools.partial(_mixer_prompt_kernel, tiles_per_seq=tps),
        grid=(t // r,),
        in_specs=[pl.BlockSpec((r, d), lambda i: (i, 0)),
                  pl.BlockSpec((6, None, 1, d), lambda i: (0, i // tps, 0, 0))] + _mixer_weight_specs(lw),
        out_specs=[pl.BlockSpec((r, d), lambda i: (i, 0)), state(w), state(w), state(v)],
        out_shape=[jax.ShapeDtypeStruct((t, d), F32),
                   jax.ShapeDtypeStruct((batch, SUBLANES, w), F32),
                   jax.ShapeDtypeStruct((batch, SUBLANES, w), F32),
                   jax.ShapeDtypeStruct((batch, SUBLANES, v), F32)],
        compiler_params=_params(("arbitrary",), 60),
    )(x_flat, mod_p, *_mixer_weight_args(lw))


def _mixer_sample(x_flat, mod_s, lw, h0, lb0, sb0, batch):
    t, d = x_flat.shape
    w = lw["lru_conv_b"].shape[-1]
    v = lw["gnorm_sc_g"].shape[-1]
    full = lambda shape: pl.BlockSpec(shape, lambda i: (0,) * len(shape))
    return pl.pallas_call(
        _mixer_sample_kernel,
        grid=(t // batch,),
        in_specs=[pl.BlockSpec((batch, d), lambda i: (i, 0)), _const_spec(mod_s.shape)]
        + _mixer_weight_specs(lw) + [_const_spec(h0.shape), _const_spec(lb0.shape), _const_spec(sb0.shape)],
        out_specs=[pl.BlockSpec((batch, d), lambda i: (i, 0)), full(h0.shape), full(lb0.shape), full(sb0.shape)],
        out_shape=[jax.ShapeDtypeStruct((t, d), F32),
                   jax.ShapeDtypeStruct(h0.shape, F32),
                   jax.ShapeDtypeStruct(lb0.shape, F32),
                   jax.ShapeDtypeStruct(sb0.shape, F32)],
        compiler_params=_params(("arbitrary",), 60),
    )(x_flat, mod_s, *_mixer_weight_args(lw), h0, lb0, sb0)


def _top_values(work, k, with_rank=False):
    kp = -(-k // SUBLANES) * SUBLANES
    row = lax.broadcasted_iota(jnp.int32, (kp, 1), 0)
    out = jnp.full((kp, work.shape[1]), -jnp.inf, F32)
    rank = jnp.full(work.shape, float(work.shape[0]), F32) if with_rank else None
    for j in range(k):
        m = jnp.max(work, axis=0, keepdims=True)
        out = jnp.where(row == j, m, out)
        hit = work == m
        if with_rank:
            rank = jnp.where(hit, float(j), rank)
        if j + 1 < k:
            work = jnp.where(hit, -jnp.inf, work)
    return (out, rank) if with_rank else out


def _prep_kernel(x1_ref, mod_ref, n2g_ref, wq_ref, sk_ref, h2t_ref, kc_ref, e1_ref, r2_ref, e2_ref):
    nk = sk_ref.shape[1]
    n_heads = wq_ref.shape[1] // (2 * nk)
    k = PEER_TOPK
    shift2, scale2 = mod_ref[3], mod_ref[4]
    h2 = _rms(x1_ref[...], n2g_ref[...]) * (1.0 + scale2) + shift2
    h2t_ref[...] = h2.T.astype(BF16)
    qb = _dot(h2.astype(BF16), wq_ref[...]).astype(BF16)
    nt = (((1,), (1,)), ((), ()))
    for hd in range(n_heads):
        s1, s2 = [lax.dot_general(sk_ref[p], qb[:, (2 * hd + p) * nk:(2 * hd + p + 1) * nk], nt,
                                  preferred_element_type=F32) for p in range(2)]
        v1 = _top_values(s1, k)
        v2, rank2 = _top_values(s2, k, with_rank=True)
        blocks = [v1[0:1, :] + v2]
        for a in range(1, SUBLANES):
            blocks.append(v1[a:a + 1, :] + v2[0:SUBLANES, :])
        blocks.append(v1[SUBLANES:, :] + v2[0:1, :])
        top = _top_values(jnp.concatenate(blocks, axis=0), k)
        z = jnp.sum(jnp.exp(top - top[0:1, :]), axis=0, keepdims=True)
        tau = top[k - 1:k, :]
        count = jnp.zeros_like(s1)
        for b in range(k):
            count = count + jnp.where(s1 + v2[b:b + 1, :] >= tau, 1.0, 0.0)
        kc_ref[hd] = count
        e1_ref[hd] = jnp.exp(s1 - v1[0:1, :]) * (1.0 / z)
        r2_ref[hd * nk:(hd + 1) * nk, :] = rank2.astype(BF16)
        e2_ref[hd * nk:(hd + 1) * nk, :] = jnp.exp(s2 - v2[0:1, :]).astype(BF16)


def _peer_prep(x1_flat, mod, mod_spec, rows, lw):
    t, d = x1_flat.shape
    nq = lw["peer_wq"].shape[1]
    nk = lw["peer_sub_keys"].shape[1]
    n_heads = nq // (2 * nk)
    return pl.pallas_call(
        _prep_kernel,
        grid=(t // rows,),
        in_specs=[pl.BlockSpec((rows, d), lambda i: (i, 0)), mod_spec,
                  _const_spec(lw["norm2_g"].shape), _const_spec(lw["peer_wq"].shape),
                  _const_spec(lw["peer_sub_keys"].shape)],
        out_specs=[pl.BlockSpec((d, rows), lambda i: (0, i)),
                   pl.BlockSpec((n_heads, nk, rows), lambda i: (0, 0, i)),
                   pl.BlockSpec((n_heads, nk, rows), lambda i: (0, 0, i)),
                   pl.BlockSpec((n_heads * nk, rows), lambda i: (0, i)),
                   pl.BlockSpec((n_heads * nk, rows), lambda i: (0, i))],
        out_shape=[jax.ShapeDtypeStruct((d, t), BF16),
                   jax.ShapeDtypeStruct((n_heads, nk, t), F32),
                   jax.ShapeDtypeStruct((n_heads, nk, t), F32),
                   jax.ShapeDtypeStruct((n_heads * nk, t), BF16),
                   jax.ShapeDtypeStruct((n_heads * nk, t), BF16)],
        compiler_params=_params(("arbitrary",), 48),
    )(x1_flat, mod, lw["norm2_g"], lw["peer_wq"], lw["peer_sub_keys"])


def _peer_kernel(h2t_ref, kc_ref, e1_ref, r2_ref, e2_ref, u_ref, vt_ref, out_ref,
                 bc_ref, s0_ref, s1_ref, a_ref, *, n_heads, nk, n_chunks):
    g = pl.program_id(0)
    c_gate = jnp.maximum(g - 1, 0) % n_chunks
    ec, tb = s0_ref.shape
    i_group = bc_ref.shape[0]
    s_rows = i_group * nk
    jrows = bc_ref.shape[2]
    gl = min(PEER_GATE_LANES, tb)

    @pl.when(g == 0)
    def _():
        s1_ref[...] = jnp.zeros_like(s1_ref)

    @pl.when(c_gate == 0)
    def _():
        out_ref[...] = jnp.zeros_like(out_ref)

    def run(s_w, s_r):
        def gate_piece(r1, jq, l0):
            tiles = [jq * PEER_J_TILES + jt for jt in range(PEER_J_TILES)]
            gates = [[jnp.zeros((jrows, gl), BF16) for _ in tiles] for _ in range(i_group)]
            for hd in range(n_heads):
                kcb = [bc_ref[ii, 2 * hd, :, l0:l0 + gl] for ii in range(i_group)]
                e1b = [bc_ref[ii, 2 * hd + 1, :, l0:l0 + gl] for ii in range(i_group)]
                for n, jt in enumerate(tiles):
                    j0 = hd * nk + jt * jrows
                    r2 = r2_ref[j0:j0 + jrows, l0:l0 + gl]
                    e2 = e2_ref[j0:j0 + jrows, l0:l0 + gl]
                    for ii in range(i_group):
                        gates[ii][n] = gates[ii][n] + jnp.where(r2 < kcb[ii], e2 * e1b[ii], 0.0)
            for ii in range(i_group):
                for n, jt in enumerate(tiles):
                    row = r1 + ii * nk + jt * jrows
                    act = jax.nn.gelu(s_r[row:row + jrows, l0:l0 + gl]).astype(BF16) * gates[ii][n]
                    a_ref[row:row + jrows, l0:l0 + gl] = act

        s_w[...] = _dot(u_ref[...], h2t_ref[...])
        for it in range(ec // s_rows):
            r1 = it * s_rows
            for ii in range(i_group):
                il = it * i_group + ii
                for hd in range(n_heads):
                    bc_ref[ii, 2 * hd] = jnp.broadcast_to(kc_ref[hd, il:il + 1, :].astype(BF16), (jrows, tb))
                    bc_ref[ii, 2 * hd + 1] = jnp.broadcast_to(e1_ref[hd, il:il + 1, :].astype(BF16), (jrows, tb))
            for jq, l0 in itertools.product(range(nk // (jrows * PEER_J_TILES)), range(0, tb, gl)):
                gate_piece(r1, jq, l0)
            out_ref[...] += _dot(vt_ref[:, r1:r1 + s_rows], a_ref[r1:r1 + s_rows, :])

    @pl.when(g % 2 == 0)
    def _():
        run(s0_ref, s1_ref)

    @pl.when(g % 2 == 1)
    def _():
        run(s1_ref, s0_ref)


def _peer_experts(h2t, kc, e1, r2, e2, u_bf, vt_bf):
    d, t = h2t.shape
    n_heads, nk, _ = kc.shape
    ne = u_bf.shape[0]
    tb = min(PEER_TOKENS, t)
    ec = PEER_EXPERTS
    nc = ne // ec
    n = (t // tb) * nc
    gate_step = lambda g: jnp.maximum(g - 1, 0)
    chunk_tab = pl.BlockSpec((n_heads, ec // nk, tb), lambda g: (0, gate_step(g) % nc, gate_step(g) // nc))
    token_tab = pl.BlockSpec((n_heads * nk, tb), lambda g: (0, gate_step(g) // nc),
                             pipeline_mode=pl.Buffered(1))
    return pl.pallas_call(
        functools.partial(_peer_kernel, n_heads=n_heads, nk=nk, n_chunks=nc),
        grid=(n + 1,),
        in_specs=[pl.BlockSpec((d, tb), lambda g: (0, jnp.minimum(g, n - 1) // nc),
                               pipeline_mode=pl.Buffered(1)),
                  chunk_tab, chunk_tab, token_tab, token_tab,
                  pl.BlockSpec((ec, d), lambda g: (jnp.minimum(g, n - 1) % nc, 0)),
                  pl.BlockSpec((None, d, ec), lambda g: (gate_step(g) % nc, 0, 0))],
        out_specs=pl.BlockSpec((d, tb), lambda g: (0, gate_step(g) // nc)),
        out_shape=jax.ShapeDtypeStruct((d, t), F32),
        scratch_shapes=[pltpu.VMEM((PEER_I_GROUP, 2 * n_heads, 2 * SUBLANES, tb), BF16),
                        pltpu.VMEM((ec, tb), F32),
                        pltpu.VMEM((ec, tb), F32),
                        pltpu.VMEM((ec, tb), BF16)],
        compiler_params=_params(("arbitrary",), 52),
    )(h2t, kc, e1, r2, e2, u_bf, vt_bf)


def _epilogue_kernel(pt_ref, x1_ref, mod_ref, fg_ref, y_ref, *, final_norm):
    xo = x1_ref[...] + mod_ref[5] * pt_ref[...].T
    y_ref[...] = _rms(xo, fg_ref[...]) if final_norm else xo


def _epilogue(peer_t, x1_flat, mod, mod_spec, rows, final_g, final_norm):
    t, d = x1_flat.shape
    return pl.pallas_call(
        functools.partial(_epilogue_kernel, final_norm=final_norm),
        grid=(t // rows,),
        in_specs=[pl.BlockSpec((d, rows), lambda i: (0, i)),
                  pl.BlockSpec((rows, d), lambda i: (i, 0)), mod_spec, _const_spec(final_g.shape)],
        out_specs=pl.BlockSpec((rows, d), lambda i: (i, 0)),
        out_shape=jax.ShapeDtypeStruct((t, d), F32),
        compiler_params=_params(("arbitrary",), 40),
    )(peer_t, x1_flat, mod, final_g)


def _block_diag(w):
    heads, hd, _ = w.shape
    per = MXU_DIM // hd
    wg = w.reshape(heads // per, per, hd, hd)
    eye = jnp.eye(per, dtype=w.dtype)
    return jnp.einsum("gpij,pq->gpiqj", wg, eye).reshape(heads // per, MXU_DIM, MXU_DIM).astype(BF16)


def kernel(x_prompt, x_sample, c_prompt, c_sample, state_lru_h, state_lru_conv, state_sconv, w_ada, b_ada, norm1_g, norm2_g, w_in, lru_conv_w, lru_conv_b, lru_wa, lru_ba, lru_wx, lru_bx, lru_lambda, sconv_w, gnorm_lru_g, gnorm_sc_g, w_out, peer_wq, peer_sub_keys, peer_u, peer_v, final_g):
    bp, seq, d = x_prompt.shape
    bs, sseq, _ = x_sample.shape
    depth = w_ada.shape[0]
    nk = peer_sub_keys.shape[2]
    n_heads = peer_wq.shape[2] // (2 * nk)
    tps = seq // PROMPT_TILE

    xp = x_prompt.reshape(bp * seq, d)
    xs = jnp.swapaxes(x_sample, 0, 1).reshape(sseq * bs, d)
    n_c = bp + bs
    pad = (-n_c) % SUBLANES
    c_all = jnp.concatenate([c_prompt, c_sample, jnp.zeros((pad, d), F32)], axis=0)
    fg = final_g.reshape(1, d)

    mod_p_spec = pl.BlockSpec((6, None, 1, d), lambda i: (0, i // tps, 0, 0))
    mod_s_spec = _const_spec((6, bs, d))

    outs = [[] for _ in range(6)]
    for l in range(depth):
        row = lambda a: a[l].reshape(1, -1)
        lw = dict(
            norm1_g=row(norm1_g), norm2_g=row(norm2_g), w_in=w_in[l].astype(BF16),
            lru_conv_w=lru_conv_w[l], lru_conv_b=row(lru_conv_b),
            wa_bd=_block_diag(lru_wa[l]), lru_ba=row(lru_ba),
            wx_bd=_block_diag(lru_wx[l]), lru_bx=row(lru_bx), lru_lambda=row(lru_lambda),
            sconv_w=sconv_w[l], gnorm_lru_g=row(gnorm_lru_g), gnorm_sc_g=row(gnorm_sc_g),
            w_out=w_out[l].astype(BF16), peer_wq=peer_wq[l].astype(BF16),
            peer_sub_keys=peer_sub_keys[l].astype(BF16))
        u_bf = peer_u[l].astype(BF16)
        ne = peer_v.shape[1]
        vt_bf = jnp.swapaxes(peer_v[l].reshape(ne // PEER_EXPERTS, PEER_EXPERTS, d), 1, 2).astype(BF16)

        mod = _adaln(c_all, w_ada[l], b_ada[l])
        mod_p = jnp.swapaxes(mod[:bp].reshape(bp, 6, d), 0, 1).reshape(6, bp, 1, d)
        mod_s = jnp.swapaxes(mod[bp:n_c].reshape(bs, 6, d), 0, 1)

        xp1, hl, lb, sb = _mixer_prompt(xp, mod_p, lw, bp, seq)
        h0 = state_lru_h[l]
        lb0 = jnp.swapaxes(state_lru_conv[l], 0, 1)
        sb0 = jnp.swapaxes(state_sconv[l], 0, 1)
        xs1, hs, lbs, sbs = _mixer_sample(xs, mod_s, lw, h0, lb0, sb0, bs)

        last = l == depth - 1
        new_x = []
        for x1, mod_g, spec, rows in ((xp1, mod_p, mod_p_spec, PROMPT_TILE), (xs1, mod_s, mod_s_spec, bs)):
            peer_t = _peer_experts(*_peer_prep(x1, mod_g, spec, rows, lw), u_bf, vt_bf)
            new_x.append(_epilogue(peer_t, x1, mod_g, spec, rows, fg, last))
        xp, xs = new_x

        nl = lru_conv_w.shape[1] - 1
        ns = sconv_w.shape[1] - 1
        outs[0].append(hl[:, SUBLANES - 1, :])
        outs[1].append(lb[:, SUBLANES - nl:, :])
        outs[2].append(sb[:, SUBLANES - ns:, :])
        outs[3].append(hs)
        outs[4].append(jnp.swapaxes(lbs, 0, 1))
        outs[5].append(jnp.swapaxes(sbs, 0, 1))

    y_prompt = xp.reshape(bp, seq, d)
    y_sample = jnp.swapaxes(xs.reshape(sseq, bs, d), 0, 1)
    return (y_prompt, y_sample) + tuple(jnp.stack(o) for o in outs)
```

```python
import functools
import itertools

import jax
import jax.numpy as jnp
from jax import lax
from jax.experimental import pallas as pl
from jax.experimental.pallas import tpu as pltpu

F32 = jnp.float32
BF16 = jnp.bfloat16

EPS = 1e-6
LRU_C = 8.0
PEER_TOPK = 16

LANES = 128
SUBLANES = 8
MXU_DIM = 256
VMEM_BYTES = 64 * 1024 * 1024

PROMPT_TILE = 256
PEER_TOKENS = 512
PEER_EXPERTS = 1024
PEER_I_GROUP = 2
PEER_GATE_LANES = 256
PEER_J_TILES = 2
ADALN_COLS = 1024


def _dot(a, b):
    return jnp.dot(a, b, preferred_element_type=F32)


def _rms(x, g):
    ms = jnp.mean(x * x, axis=-1, keepdims=True)
    return x * lax.rsqrt(ms + EPS) * g


def _const_spec(shape):
    nd = len(shape)
    return pl.BlockSpec(shape, lambda *_: (0,) * nd, pipeline_mode=pl.Buffered(1))


def _params(sem, vmem_mb):
    return pltpu.CompilerParams(dimension_semantics=sem, vmem_limit_bytes=vmem_mb * 1024 * 1024)


def _adaln_kernel(c_ref, w_ref, b_ref, o_ref):
    sc = jax.nn.silu(c_ref[...]).astype(BF16)
    o_ref[...] = _dot(sc, w_ref[...].astype(BF16)) + b_ref[...]


def _adaln(c_all, w_ada, b_ada):
    m, d = c_all.shape
    n = w_ada.shape[1]
    tn = ADALN_COLS
    return pl.pallas_call(
        _adaln_kernel,
        grid=(n // tn,),
        in_specs=[pl.BlockSpec((m, d), lambda j: (0, 0)),
                  pl.BlockSpec((d, tn), lambda j: (0, j)),
                  pl.BlockSpec((1, tn), lambda j: (0, j))],
        out_specs=pl.BlockSpec((m, tn), lambda j: (0, j)),
        out_shape=jax.ShapeDtypeStruct((m, n), F32),
        compiler_params=_params(("arbitrary",), 40),
    )(c_all, w_ada, b_ada.reshape(1, n))


def _gate_pre(xcb, w_ref, b_ref):
    nblk = w_ref.shape[0]
    parts = [_dot(xcb[:, g * MXU_DIM:(g + 1) * MXU_DIM], w_ref[g]) for g in range(nblk)]
    return jnp.concatenate(parts, axis=1) + b_ref[...]


def _scan_rows(a, u, h_in):
    rows = a.shape[0]
    row = lax.broadcasted_iota(jnp.int32, (rows, 1), 0)
    acc_a = a
    acc_b = u + jnp.where(row == 0, a * h_in, 0.0)
    d = 1
    while d < rows:
        keep = row >= d
        sh_b = pltpu.roll(acc_b, d, axis=0)
        acc_b = acc_b + jnp.where(keep, acc_a * sh_b, 0.0)
        if 2 * d < rows:
            sh_a = pltpu.roll(acc_a, d, axis=0)
            acc_a = jnp.where(keep, acc_a * sh_a, acc_a)
        d *= 2
    return acc_b


def _mixer_core(x, mod_ref, n1g_ref, win_ref, cb_ref, wa_ref, ba_ref, wx_ref, bx_ref, lam_ref,
                gl_ref, gs_ref, wout_ref, lru_conv, lru_scan, sc_conv):
    w = cb_ref.shape[-1]
    v = gs_ref.shape[-1]
    shift1, scale1, gate1 = mod_ref[0], mod_ref[1], mod_ref[2]
    h = _rms(x, n1g_ref[...]) * (1.0 + scale1) + shift1
    hb = h.astype(BF16)

    x_lru = _dot(hb, win_ref[:, 0:w])
    xc = lru_conv(x_lru) + cb_ref[...]
    xcb = xc.astype(BF16)
    r = jax.nn.sigmoid(_gate_pre(xcb, wa_ref, ba_ref))
    i = jax.nn.sigmoid(_gate_pre(xcb, wx_ref, bx_ref))
    log_a = (-LRU_C) * r * jax.nn.softplus(-lam_ref[...])
    a = jnp.exp(log_a)
    u = jnp.sqrt(jnp.tanh(-log_a) * (1.0 + a * a)) * (i * xc)
    rec = lru_scan(a, u)
    y_gate = _dot(hb, win_ref[:, w:2 * w])
    out_lru = rec * jax.nn.gelu(y_gate)

    sc_b = _dot(hb, win_ref[:, 2 * w:2 * w + v])
    sc_c = _dot(hb, win_ref[:, 2 * w + v:2 * w + 2 * v])
    sc_x = _dot(hb, win_ref[:, 2 * w + 2 * v:2 * w + 3 * v])
    cx = sc_c * sc_x
    out_sc = sc_b * sc_conv(cx)

    n_l = _rms(out_lru, gl_ref[...]).astype(BF16)
    n_s = _rms(out_sc, gs_ref[...]).astype(BF16)
    mix = _dot(n_l, wout_ref[0:w, :]) + _dot(n_s, wout_ref[w:w + v, :])
    return x + gate1 * mix, x_lru, rec, cx


def _mixer_prompt_kernel(x_ref, mod_ref, n1g_ref, win_ref, cw_ref, cb_ref, wa_ref, ba_ref, wx_ref,
                         bx_ref, lam_ref, sw_ref, gl_ref, gs_ref, wout_ref,
                         x1_ref, hl_ref, lb_ref, sb_ref, *, tiles_per_seq):
    rows = x_ref.shape[0]
    hdr = SUBLANES

    @pl.when(pl.program_id(0) % tiles_per_seq == 0)
    def _():
        hl_ref[...] = jnp.zeros_like(hl_ref)
        lb_ref[...] = jnp.zeros_like(lb_ref)
        sb_ref[...] = jnp.zeros_like(sb_ref)

    def conv(cur, tail_ref, w_ref):
        width = w_ref.shape[0]
        ext = jnp.concatenate([tail_ref[...], cur], axis=0)
        out = w_ref[width - 1:width, :] * cur
        for d in range(1, width):
            out = out + w_ref[width - 1 - d:width - d, :] * pltpu.roll(ext, d, axis=0)[hdr:]
        return out

    new_x, x_lru, rec, cx = _mixer_core(
        x_ref[...], mod_ref, n1g_ref, win_ref, cb_ref, wa_ref, ba_ref, wx_ref, bx_ref, lam_ref,
        gl_ref, gs_ref, wout_ref,
        lru_conv=lambda cur: conv(cur, lb_ref, cw_ref),
        lru_scan=lambda a, u: _scan_rows(a, u, hl_ref[hdr - 1:hdr, :]),
        sc_conv=lambda cur: conv(cur, sb_ref, sw_ref))
    x1_ref[...] = new_x
    hl_ref[...] = rec[rows - hdr:, :]
    lb_ref[...] = x_lru[rows - hdr:, :]
    sb_ref[...] = cx[rows - hdr:, :]


def _mixer_sample_kernel(x_ref, mod_ref, n1g_ref, win_ref, cw_ref, cb_ref, wa_ref, ba_ref, wx_ref,
                         bx_ref, lam_ref, sw_ref, gl_ref, gs_ref, wout_ref, h0_ref, lb0_ref, sb0_ref,
                         x1_ref, h_ref, lb_ref, sb_ref):
    @pl.when(pl.program_id(0) == 0)
    def _():
        h_ref[...] = h0_ref[...]
        lb_ref[...] = lb0_ref[...]
        sb_ref[...] = sb0_ref[...]

    def conv(cur, hist_ref, w_ref):
        width = w_ref.shape[0]
        out = w_ref[width - 1:width, :] * cur
        for k in range(width - 1):
            out = out + w_ref[k:k + 1, :] * hist_ref[k]
        for k in range(width - 2):
            hist_ref[k] = hist_ref[k + 1]
        hist_ref[width - 2] = cur
        return out

    def scan(a, u):
        h = a * h_ref[...] + u
        h_ref[...] = h
        return h

    new_x, _, _, _ = _mixer_core(
        x_ref[...], mod_ref, n1g_ref, win_ref, cb_ref, wa_ref, ba_ref, wx_ref, bx_ref, lam_ref,
        gl_ref, gs_ref, wout_ref,
        lru_conv=lambda cur: conv(cur, lb_ref, cw_ref),
        lru_scan=scan,
        sc_conv=lambda cur: conv(cur, sb_ref, sw_ref))
    x1_ref[...] = new_x


def _mixer_weight_specs(lw):
    return [_const_spec(lw[k].shape) for k in
            ("norm1_g", "w_in", "lru_conv_w", "lru_conv_b", "wa_bd", "lru_ba", "wx_bd", "lru_bx",
             "lru_lambda", "sconv_w", "gnorm_lru_g", "gnorm_sc_g", "w_out")]


def _mixer_weight_args(lw):
    return [lw[k] for k in
            ("norm1_g", "w_in", "lru_conv_w", "lru_conv_b", "wa_bd", "lru_ba", "wx_bd", "lru_bx",
             "lru_lambda", "sconv_w", "gnorm_lru_g", "gnorm_sc_g", "w_out")]


def _mixer_prompt(x_flat, mod_p, lw, batch, seq):
    t, d = x_flat.shape
    w = lw["lru_conv_b"].shape[-1]
    v = lw["gnorm_sc_g"].shape[-1]
    r = PROMPT_TILE
    tps = seq // r
    state = lambda c: pl.BlockSpec((None, SUBLANES, c), lambda i: (i // tps, 0, 0))
    return pl.pallas_call(
        functools.partial(_mixer_prompt_kernel, tiles_per_seq=tps),
        grid=(t // r,),
        in_specs=[pl.BlockSpec((r, d), lambda i: (i, 0)),
                  pl.BlockSpec((6, None, 1, d), lambda i: (0, i // tps, 0, 0))] + _mixer_weight_specs(lw),
        out_specs=[pl.BlockSpec((r, d), lambda i: (i, 0)), state(w), state(w), state(v)],
        out_shape=[jax.ShapeDtypeStruct((t, d), F32),
                   jax.ShapeDtypeStruct((batch, SUBLANES, w), F32),
                   jax.ShapeDtypeStruct((batch, SUBLANES, w), F32),
                   jax.ShapeDtypeStruct((batch, SUBLANES, v), F32)],
        compiler_params=_params(("arbitrary",), 60),
    )(x_flat, mod_p, *_mixer_weight_args(lw))


def _mixer_sample(x_flat, mod_s, lw, h0, lb0, sb0, batch):
    t, d = x_flat.shape
    w = lw["lru_conv_b"].shape[-1]
    v = lw["gnorm_sc_g"].shape[-1]
    full = lambda shape: pl.BlockSpec(shape, lambda i: (0,) * len(shape))
    return pl.pallas_call(
        _mixer_sample_kernel,
        grid=(t // batch,),
        in_specs=[pl.BlockSpec((batch, d), lambda i: (i, 0)), _const_spec(mod_s.shape)]
        + _mixer_weight_specs(lw) + [_const_spec(h0.shape), _const_spec(lb0.shape), _const_spec(sb0.shape)],
        out_specs=[pl.BlockSpec((batch, d), lambda i: (i, 0)), full(h0.shape), full(lb0.shape), full(sb0.shape)],
        out_shape=[jax.ShapeDtypeStruct((t, d), F32),
                   jax.ShapeDtypeStruct(h0.shape, F32),
                   jax.ShapeDtypeStruct(lb0.shape, F32),
                   jax.ShapeDtypeStruct(sb0.shape, F32)],
        compiler_params=_params(("arbitrary",), 60),
    )(x_flat, mod_s, *_mixer_weight_args(lw), h0, lb0, sb0)


def _top_values(work, k, with_rank=False):
    kp = -(-k // SUBLANES) * SUBLANES
    row = lax.broadcasted_iota(jnp.int32, (kp, 1), 0)
    out = jnp.full((kp, work.shape[1]), -jnp.inf, F32)
    rank = jnp.full(work.shape, float(work.shape[0]), F32) if with_rank else None
    for j in range(k):
        m = jnp.max(work, axis=0, keepdims=True)
        out = jnp.where(row == j, m, out)
        hit = work == m
        if with_rank:
            rank = jnp.where(hit, float(j), rank)
        if j + 1 < k:
            work = jnp.where(hit, -jnp.inf, work)
    return (out, rank) if with_rank else out


def _prep_kernel(x1_ref, mod_ref, n2g_ref, wq_ref, sk_ref, h2t_ref, kc_ref, e1_ref, r2_ref, e2_ref):
    nk = sk_ref.shape[1]
    n_heads = wq_ref.shape[1] // (2 * nk)
    k = PEER_TOPK
    shift2, scale2 = mod_ref[3], mod_ref[4]
    h2 = _rms(x1_ref[...], n2g_ref[...]) * (1.0 + scale2) + shift2
    h2t_ref[...] = h2.T.astype(BF16)
    qb = _dot(h2.astype(BF16), wq_ref[...]).astype(BF16)
    nt = (((1,), (1,)), ((), ()))
    for hd in range(n_heads):
        s1, s2 = [lax.dot_general(sk_ref[p], qb[:, (2 * hd + p) * nk:(2 * hd + p + 1) * nk], nt,
                                  preferred_element_type=F32) for p in range(2)]
        v1 = _top_values(s1, k)
        v2, rank2 = _top_values(s2, k, with_rank=True)
        blocks = [v1[0:1, :] + v2]
        for a in range(1, SUBLANES):
            blocks.append(v1[a:a + 1, :] + v2[0:SUBLANES, :])
        blocks.append(v1[SUBLANES:, :] + v2[0:1, :])
        top = _top_values(jnp.concatenate(blocks, axis=0), k)
        z = jnp.sum(jnp.exp(top - top[0:1, :]), axis=0, keepdims=True)
        tau = top[k - 1:k, :]
        count_a = jnp.zeros_like(v1)
        for b in range(k):
            count_a = count_a + jnp.where(v1 + v2[b:b + 1, :] >= tau, 1.0, 0.0)
        count = jnp.zeros_like(s1)
        for a in range(k):
            count = jnp.where(s1 == v1[a:a + 1, :], count_a[a:a + 1, :], count)
        kc_ref[hd] = count
        e1_ref[hd] = jnp.exp(s1 - v1[0:1, :]) * (1.0 / z)
        r2_ref[hd * nk:(hd + 1) * nk, :] = rank2.astype(BF16)
        e2_ref[hd * nk:(hd + 1) * nk, :] = jnp.exp(s2 - v2[0:1, :]).astype(BF16)


def _peer_prep(x1_flat, mod, mod_spec, rows, lw):
    t, d = x1_flat.shape
    nq = lw["peer_wq"].shape[1]
    nk = lw["peer_sub_keys"].shape[1]
    n_heads = nq // (2 * nk)
    return pl.pallas_call(
        _prep_kernel,
        grid=(t // rows,),
        in_specs=[pl.BlockSpec((rows, d), lambda i: (i, 0)), mod_spec,
                  _const_spec(lw["norm2_g"].shape), _const_spec(lw["peer_wq"].shape),
                  _const_spec(lw["peer_sub_keys"].shape)],
        out_specs=[pl.BlockSpec((d, rows), lambda i: (0, i)),
                   pl.BlockSpec((n_heads, nk, rows), lambda i: (0, 0, i)),
                   pl.BlockSpec((n_heads, nk, rows), lambda i: (0, 0, i)),
                   pl.BlockSpec((n_heads * nk, rows), lambda i: (0, i)),
                   pl.BlockSpec((n_heads * nk, rows), lambda i: (0, i))],
        out_shape=[jax.ShapeDtypeStruct((d, t), BF16),
                   jax.ShapeDtypeStruct((n_heads, nk, t), F32),
                   jax.ShapeDtypeStruct((n_heads, nk, t), F32),
                   jax.ShapeDtypeStruct((n_heads * nk, t), BF16),
                   jax.ShapeDtypeStruct((n_heads * nk, t), BF16)],
        compiler_params=_params(("arbitrary",), 48),
    )(x1_flat, mod, lw["norm2_g"], lw["peer_wq"], lw["peer_sub_keys"])


def _peer_kernel(h2t_ref, kc_ref, e1_ref, r2_ref, e2_ref, u_ref, vt_ref, out_ref,
                 bc_ref, s0_ref, s1_ref, a_ref, *, n_heads, nk, n_chunks):
    g = pl.program_id(0)
    c_gate = jnp.maximum(g - 1, 0) % n_chunks
    ec, tb = s0_ref.shape
    i_group = bc_ref.shape[0]
    s_rows = i_group * nk
    jrows = bc_ref.shape[2]
    gl = min(PEER_GATE_LANES, tb)

    @pl.when(g == 0)
    def _():
        s1_ref[...] = jnp.zeros_like(s1_ref)

    @pl.when(c_gate == 0)
    def _():
        out_ref[...] = jnp.zeros_like(out_ref)

    def run(s_w, s_r):
        def gate_piece(r1, jq, l0):
            tiles = [jq * PEER_J_TILES + jt for jt in range(PEER_J_TILES)]
            gates = [[jnp.zeros((jrows, gl), BF16) for _ in tiles] for _ in range(i_group)]
            for hd in range(n_heads):
                kcb = [bc_ref[ii, 2 * hd, :, l0:l0 + gl] for ii in range(i_group)]
                e1b = [bc_ref[ii, 2 * hd + 1, :, l0:l0 + gl] for ii in range(i_group)]
                for n, jt in enumerate(tiles):
                    j0 = hd * nk + jt * jrows
                    r2 = r2_ref[j0:j0 + jrows, l0:l0 + gl]
                    e2 = e2_ref[j0:j0 + jrows, l0:l0 + gl]
                    for ii in range(i_group):
                        gates[ii][n] = gates[ii][n] + jnp.where(r2 < kcb[ii], e2 * e1b[ii], 0.0)
            for ii in range(i_group):
                for n, jt in enumerate(tiles):
                    row = r1 + ii * nk + jt * jrows
                    act = jax.nn.gelu(s_r[row:row + jrows, l0:l0 + gl]) * gates[ii][n]
                    a_ref[row:row + jrows, l0:l0 + gl] = act

        s_w[...] = _dot(u_ref[...], h2t_ref[...]).astype(BF16)
        for it in range(ec // s_rows):
            r1 = it * s_rows
            for ii in range(i_group):
                il = it * i_group + ii
                for hd in range(n_heads):
                    bc_ref[ii, 2 * hd] = jnp.broadcast_to(kc_ref[hd, il:il + 1, :].astype(BF16), (jrows, tb))
                    bc_ref[ii, 2 * hd + 1] = jnp.broadcast_to(e1_ref[hd, il:il + 1, :].astype(BF16), (jrows, tb))
            for jq, l0 in itertools.product(range(nk // (jrows * PEER_J_TILES)), range(0, tb, gl)):
                gate_piece(r1, jq, l0)
            out_ref[...] += _dot(vt_ref[:, r1:r1 + s_rows], a_ref[r1:r1 + s_rows, :])

    @pl.when(g % 2 == 0)
    def _():
        run(s0_ref, s1_ref)

    @pl.when(g % 2 == 1)
    def _():
        run(s1_ref, s0_ref)


def _peer_experts(h2t, kc, e1, r2, e2, u_bf, vt_bf):
    d, t = h2t.shape
    n_heads, nk, _ = kc.shape
    ne = u_bf.shape[0]
    tb = min(PEER_TOKENS, t)
    ec = PEER_EXPERTS
    nc = ne // ec
    n = (t // tb) * nc
    gate_step = lambda g: jnp.maximum(g - 1, 0)
    chunk_tab = pl.BlockSpec((n_heads, ec // nk, tb), lambda g: (0, gate_step(g) % nc, gate_step(g) // nc))
    token_tab = pl.BlockSpec((n_heads * nk, tb), lambda g: (0, gate_step(g) // nc),
                             pipeline_mode=pl.Buffered(1))
    return pl.pallas_call(
        functools.partial(_peer_kernel, n_heads=n_heads, nk=nk, n_chunks=nc),
        grid=(n + 1,),
        in_specs=[pl.BlockSpec((d, tb), lambda g: (0, jnp.minimum(g, n - 1) // nc),
                               pipeline_mode=pl.Buffered(1)),
                  chunk_tab, chunk_tab, token_tab, token_tab,
                  pl.BlockSpec((ec, d), lambda g: (jnp.minimum(g, n - 1) % nc, 0)),
                  pl.BlockSpec((None, d, ec), lambda g: (gate_step(g) % nc, 0, 0))],
        out_specs=pl.BlockSpec((d, tb), lambda g: (0, gate_step(g) // nc)),
        out_shape=jax.ShapeDtypeStruct((d, t), F32),
        scratch_shapes=[pltpu.VMEM((PEER_I_GROUP, 2 * n_heads, 2 * SUBLANES, tb), BF16),
                        pltpu.VMEM((ec, tb), BF16),
                        pltpu.VMEM((ec, tb), BF16),
                        pltpu.VMEM((ec, tb), BF16)],
        compiler_params=_params(("arbitrary",), 52),
    )(h2t, kc, e1, r2, e2, u_bf, vt_bf)


def _epilogue_kernel(pt_ref, x1_ref, mod_ref, fg_ref, y_ref, *, final_norm):
    xo = x1_ref[...] + mod_ref[5] * pt_ref[...].T
    y_ref[...] = _rms(xo, fg_ref[...]) if final_norm else xo


def _epilogue(peer_t, x1_flat, mod, mod_spec, rows, final_g, final_norm):
    t, d = x1_flat.shape
    return pl.pallas_call(
        functools.partial(_epilogue_kernel, final_norm=final_norm),
        grid=(t // rows,),
        in_specs=[pl.BlockSpec((d, rows), lambda i: (0, i)),
                  pl.BlockSpec((rows, d), lambda i: (i, 0)), mod_spec, _const_spec(final_g.shape)],
        out_specs=pl.BlockSpec((rows, d), lambda i: (i, 0)),
        out_shape=jax.ShapeDtypeStruct((t, d), F32),
        compiler_params=_params(("arbitrary",), 40),
    )(peer_t, x1_flat, mod, final_g)


def _block_diag(w):
    heads, hd, _ = w.shape
    per = MXU_DIM // hd
    wg = w.reshape(heads // per, per, hd, hd)
    eye = jnp.eye(per, dtype=w.dtype)
    return jnp.einsum("gpij,pq->gpiqj", wg, eye).reshape(heads // per, MXU_DIM, MXU_DIM).astype(BF16)


def kernel(x_prompt, x_sample, c_prompt, c_sample, state_lru_h, state_lru_conv, state_sconv, w_ada, b_ada, norm1_g, norm2_g, w_in, lru_conv_w, lru_conv_b, lru_wa, lru_ba, lru_wx, lru_bx, lru_lambda, sconv_w, gnorm_lru_g, gnorm_sc_g, w_out, peer_wq, peer_sub_keys, peer_u, peer_v, final_g):
    bp, seq, d = x_prompt.shape
    bs, sseq, _ = x_sample.shape
    depth = w_ada.shape[0]
    nk = peer_sub_keys.shape[2]
    n_heads = peer_wq.shape[2] // (2 * nk)
    tps = seq // PROMPT_TILE

    xp = x_prompt.reshape(bp * seq, d)
    xs = jnp.swapaxes(x_sample, 0, 1).reshape(sseq * bs, d)
    n_c = bp + bs
    pad = (-n_c) % SUBLANES
    c_all = jnp.concatenate([c_prompt, c_sample, jnp.zeros((pad, d), F32)], axis=0)
    fg = final_g.reshape(1, d)

    mod_p_spec = pl.BlockSpec((6, None, 1, d), lambda i: (0, i // tps, 0, 0))
    mod_s_spec = _const_spec((6, bs, d))

    outs = [[] for _ in range(6)]
    for l in range(depth):
        row = lambda a: a[l].reshape(1, -1)
        lw = dict(
            norm1_g=row(norm1_g), norm2_g=row(norm2_g), w_in=w_in[l].astype(BF16),
            lru_conv_w=lru_conv_w[l], lru_conv_b=row(lru_conv_b),
            wa_bd=_block_diag(lru_wa[l]), lru_ba=row(lru_ba),
            wx_bd=_block_diag(lru_wx[l]), lru_bx=row(lru_bx), lru_lambda=row(lru_lambda),
            sconv_w=sconv_w[l], gnorm_lru_g=row(gnorm_lru_g), gnorm_sc_g=row(gnorm_sc_g),
            w_out=w_out[l].astype(BF16), peer_wq=peer_wq[l].astype(BF16),
            peer_sub_keys=peer_sub_keys[l].astype(BF16))
        u_bf = peer_u[l].astype(BF16)
        ne = peer_v.shape[1]
        vt_bf = jnp.swapaxes(peer_v[l].reshape(ne // PEER_EXPERTS, PEER_EXPERTS, d), 1, 2).astype(BF16)

        mod = _adaln(c_all, w_ada[l], b_ada[l])
        mod_p = jnp.swapaxes(mod[:bp].reshape(bp, 6, d), 0, 1).reshape(6, bp, 1, d)
        mod_s = jnp.swapaxes(mod[bp:n_c].reshape(bs, 6, d), 0, 1)

        xp1, hl, lb, sb = _mixer_prompt(xp, mod_p, lw, bp, seq)
        h0 = state_lru_h[l]
        lb0 = jnp.swapaxes(state_lru_conv[l], 0, 1)
        sb0 = jnp.swapaxes(state_sconv[l], 0, 1)
        xs1, hs, lbs, sbs = _mixer_sample(xs, mod_s, lw, h0, lb0, sb0, bs)

        last = l == depth - 1
        new_x = []
        for x1, mod_g, spec, rows in ((xp1, mod_p, mod_p_spec, PROMPT_TILE), (xs1, mod_s, mod_s_spec, bs)):
            peer_t = _peer_experts(*_peer_prep(x1, mod_g, spec, rows, lw), u_bf, vt_bf)
            new_x.append(_epilogue(peer_t, x1, mod_g, spec, rows, fg, last))
        xp, xs = new_x

        nl = lru_conv_w.shape[1] - 1
        ns = sconv_w.shape[1] - 1
        outs[0].append(hl[:, SUBLANES - 1, :])
        outs[1].append(lb[:, SUBLANES - nl:, :])
        outs[2].append(sb[:, SUBLANES - ns:, :])
        outs[3].append(hs)
        outs[4].append(jnp.swapaxes(lbs, 0, 1))
        outs[5].append(jnp.swapaxes(sbs, 0, 1))

    y_prompt = xp.reshape(bp, seq, d)
    y_sample = jnp.swapaxes(xs.reshape(sseq, bs, d), 0, 1)
    return (y_prompt, y_sample) + tuple(jnp.stack(o) for o in outs)
```

```python
import functools
import itertools

import jax
import jax.numpy as jnp
from jax import lax
from jax.experimental import pallas as pl
from jax.experimental.pallas import tpu as pltpu

F32 = jnp.float32
BF16 = jnp.bfloat16

EPS = 1e-6
LRU_C = 8.0
PEER_TOPK = 16

LANES = 128
SUBLANES = 8
MXU_DIM = 256
VMEM_BYTES = 64 * 1024 * 1024

PROMPT_TILE = 256
PEER_TOKENS = 1024
PEER_EXPERTS = 1024
PEER_I_GROUP = 2
PEER_GATE_LANES = 256
PEER_J_TILES = 2
ADALN_COLS = 1024


def _dot(a, b):
    return jnp.dot(a, b, preferred_element_type=F32)


def _rms(x, g):
    ms = jnp.mean(x * x, axis=-1, keepdims=True)
    return x * lax.rsqrt(ms + EPS) * g


def _const_spec(shape):
    nd = len(shape)
    return pl.BlockSpec(shape, lambda *_: (0,) * nd, pipeline_mode=pl.Buffered(1))


def _params(sem, vmem_mb):
    return pltpu.CompilerParams(dimension_semantics=sem, vmem_limit_bytes=vmem_mb * 1024 * 1024)


def _adaln_kernel(c_ref, w_ref, b_ref, o_ref):
    sc = jax.nn.silu(c_ref[...]).astype(BF16)
    o_ref[...] = _dot(sc, w_ref[...].astype(BF16)) + b_ref[...]


def _adaln(c_all, w_ada, b_ada):
    m, d = c_all.shape
    n = w_ada.shape[1]
    tn = ADALN_COLS
    return pl.pallas_call(
        _adaln_kernel,
        grid=(n // tn,),
        in_specs=[pl.BlockSpec((m, d), lambda j: (0, 0)),
                  pl.BlockSpec((d, tn), lambda j: (0, j)),
                  pl.BlockSpec((1, tn), lambda j: (0, j))],
        out_specs=pl.BlockSpec((m, tn), lambda j: (0, j)),
        out_shape=jax.ShapeDtypeStruct((m, n), F32),
        compiler_params=_params(("arbitrary",), 40),
    )(c_all, w_ada, b_ada.reshape(1, n))


def _gate_pre(xcb, w_ref, b_ref):
    nblk = w_ref.shape[0]
    parts = [_dot(xcb[:, g * MXU_DIM:(g + 1) * MXU_DIM], w_ref[g]) for g in range(nblk)]
    return jnp.concatenate(parts, axis=1) + b_ref[...]


def _scan_rows(a, u, h_in):
    rows = a.shape[0]
    row = lax.broadcasted_iota(jnp.int32, (rows, 1), 0)
    acc_a = a
    acc_b = u + jnp.where(row == 0, a * h_in, 0.0)
    d = 1
    while d < rows:
        keep = row >= d
        sh_b = pltpu.roll(acc_b, d, axis=0)
        acc_b = acc_b + jnp.where(keep, acc_a * sh_b, 0.0)
        if 2 * d < rows:
            sh_a = pltpu.roll(acc_a, d, axis=0)
            acc_a = jnp.where(keep, acc_a * sh_a, acc_a)
        d *= 2
    return acc_b


def _mixer_core(x, mod_ref, n1g_ref, win_ref, cb_ref, wa_ref, ba_ref, wx_ref, bx_ref, lam_ref,
                gl_ref, gs_ref, wout_ref, lru_conv, lru_scan, sc_conv):
    w = cb_ref.shape[-1]
    v = gs_ref.shape[-1]
    shift1, scale1, gate1 = mod_ref[0], mod_ref[1], mod_ref[2]
    h = _rms(x, n1g_ref[...]) * (1.0 + scale1) + shift1
    hb = h.astype(BF16)

    x_lru = _dot(hb, win_ref[:, 0:w])
    xc = lru_conv(x_lru) + cb_ref[...]
    xcb = xc.astype(BF16)
    r = jax.nn.sigmoid(_gate_pre(xcb, wa_ref, ba_ref))
    i = jax.nn.sigmoid(_gate_pre(xcb, wx_ref, bx_ref))
    log_a = (-LRU_C) * r * jax.nn.softplus(-lam_ref[...])
    a = jnp.exp(log_a)
    u = jnp.sqrt(jnp.tanh(-log_a) * (1.0 + a * a)) * (i * xc)
    rec = lru_scan(a, u)
    y_gate = _dot(hb, win_ref[:, w:2 * w])
    out_lru = rec * jax.nn.gelu(y_gate)

    sc_b = _dot(hb, win_ref[:, 2 * w:2 * w + v])
    sc_c = _dot(hb, win_ref[:, 2 * w + v:2 * w + 2 * v])
    sc_x = _dot(hb, win_ref[:, 2 * w + 2 * v:2 * w + 3 * v])
    cx = sc_c * sc_x
    out_sc = sc_b * sc_conv(cx)

    n_l = _rms(out_lru, gl_ref[...]).astype(BF16)
    n_s = _rms(out_sc, gs_ref[...]).astype(BF16)
    mix = _dot(n_l, wout_ref[0:w, :]) + _dot(n_s, wout_ref[w:w + v, :])
    return x + gate1 * mix, x_lru, rec, cx


def _mixer_prompt_kernel(x_ref, mod_ref, n1g_ref, win_ref, cw_ref, cb_ref, wa_ref, ba_ref, wx_ref,
                         bx_ref, lam_ref, sw_ref, gl_ref, gs_ref, wout_ref,
                         x1_ref, hl_ref, lb_ref, sb_ref, *, tiles_per_seq):
    rows = x_ref.shape[0]
    hdr = SUBLANES

    @pl.when(pl.program_id(0) % tiles_per_seq == 0)
    def _():
        hl_ref[...] = jnp.zeros_like(hl_ref)
        lb_ref[...] = jnp.zeros_like(lb_ref)
        sb_ref[...] = jnp.zeros_like(sb_ref)

    def conv(cur, tail_ref, w_ref):
        width = w_ref.shape[0]
        ext = jnp.concatenate([tail_ref[...], cur], axis=0)
        out = w_ref[width - 1:width, :] * cur
        for d in range(1, width):
            out = out + w_ref[width - 1 - d:width - d, :] * pltpu.roll(ext, d, axis=0)[hdr:]
        return out

    new_x, x_lru, rec, cx = _mixer_core(
        x_ref[...], mod_ref, n1g_ref, win_ref, cb_ref, wa_ref, ba_ref, wx_ref, bx_ref, lam_ref,
        gl_ref, gs_ref, wout_ref,
        lru_conv=lambda cur: conv(cur, lb_ref, cw_ref),
        lru_scan=lambda a, u: _scan_rows(a, u, hl_ref[hdr - 1:hdr, :]),
        sc_conv=lambda cur: conv(cur, sb_ref, sw_ref))
    x1_ref[...] = new_x
    hl_ref[...] = rec[rows - hdr:, :]
    lb_ref[...] = x_lru[rows - hdr:, :]
    sb_ref[...] = cx[rows - hdr:, :]


def _mixer_sample_kernel(x_ref, mod_ref, n1g_ref, win_ref, cw_ref, cb_ref, wa_ref, ba_ref, wx_ref,
                         bx_ref, lam_ref, sw_ref, gl_ref, gs_ref, wout_ref, h0_ref, lb0_ref, sb0_ref,
                         x1_ref, h_ref, lb_ref, sb_ref):
    @pl.when(pl.program_id(0) == 0)
    def _():
        h_ref[...] = h0_ref[...]
        lb_ref[...] = lb0_ref[...]
        sb_ref[...] = sb0_ref[...]

    def conv(cur, hist_ref, w_ref):
        width = w_ref.shape[0]
        out = w_ref[width - 1:width, :] * cur
        for k in range(width - 1):
            out = out + w_ref[k:k + 1, :] * hist_ref[k]
        for k in range(width - 2):
            hist_ref[k] = hist_ref[k + 1]
        hist_ref[width - 2] = cur
        return out

    def scan(a, u):
        h = a * h_ref[...] + u
        h_ref[...] = h
        return h

    new_x, _, _, _ = _mixer_core(
        x_ref[...], mod_ref, n1g_ref, win_ref, cb_ref, wa_ref, ba_ref, wx_ref, bx_ref, lam_ref,
        gl_ref, gs_ref, wout_ref,
        lru_conv=lambda cur: conv(cur, lb_ref, cw_ref),
        lru_scan=scan,
        sc_conv=lambda cur: conv(cur, sb_ref, sw_ref))
    x1_ref[...] = new_x


def _mixer_weight_specs(lw):
    return [_const_spec(lw[k].shape) for k in
            ("norm1_g", "w_in", "lru_conv_w", "lru_conv_b", "wa_bd", "lru_ba", "wx_bd", "lru_bx",
             "lru_lambda", "sconv_w", "gnorm_lru_g", "gnorm_sc_g", "w_out")]


def _mixer_weight_args(lw):
    return [lw[k] for k in
            ("norm1_g", "w_in", "lru_conv_w", "lru_conv_b", "wa_bd", "lru_ba", "wx_bd", "lru_bx",
             "lru_lambda", "sconv_w", "gnorm_lru_g", "gnorm_sc_g", "w_out")]


def _mixer_prompt(x_flat, mod_p, lw, batch, seq):
    t, d = x_flat.shape
    w = lw["lru_conv_b"].shape[-1]
    v = lw["gnorm_sc_g"].shape[-1]
    r = PROMPT_TILE
    tps = seq // r
    state = lambda c: pl.BlockSpec((None, SUBLANES, c), lambda i: (i // tps, 0, 0))
    return pl.pallas_call(
        functools.partial(_mixer_prompt_kernel, tiles_per_seq=tps),
        grid=(t // r,),
        in_specs=[pl.BlockSpec((r, d), lambda i: (i, 0)),
                  pl.BlockSpec((6, None, 1, d), lambda i: (0, i // tps, 0, 0))] + _mixer_weight_specs(lw),
        out_specs=[pl.BlockSpec((r, d), lambda i: (i, 0)), state(w), state(w), state(v)],
        out_shape=[jax.ShapeDtypeStruct((t, d), F32),
                   jax.ShapeDtypeStruct((batch, SUBLANES, w), F32),
                   jax.ShapeDtypeStruct((batch, SUBLANES, w), F32),
                   jax.ShapeDtypeStruct((batch, SUBLANES, v), F32)],
        compiler_params=_params(("arbitrary",), 60),
    )(x_flat, mod_p, *_mixer_weight_args(lw))


def _mixer_sample(x_flat, mod_s, lw, h0, lb0, sb0, batch):
    t, d = x_flat.shape
    w = lw["lru_conv_b"].shape[-1]
    v = lw["gnorm_sc_g"].shape[-1]
    full = lambda shape: pl.BlockSpec(shape, lambda i: (0,) * len(shape))
    return pl.pallas_call(
        _mixer_sample_kernel,
        grid=(t // batch,),
        in_specs=[pl.BlockSpec((batch, d), lambda i: (i, 0)), _const_spec(mod_s.shape)]
        + _mixer_weight_specs(lw) + [_const_spec(h0.shape), _const_spec(lb0.shape), _const_spec(sb0.shape)],
        out_specs=[pl.BlockSpec((batch, d), lambda i: (i, 0)), full(h0.shape), full(lb0.shape), full(sb0.shape)],
        out_shape=[jax.ShapeDtypeStruct((t, d), F32),
                   jax.ShapeDtypeStruct(h0.shape, F32),
                   jax.ShapeDtypeStruct(lb0.shape, F32),
                   jax.ShapeDtypeStruct(sb0.shape, F32)],
        compiler_params=_params(("arbitrary",), 60),
    )(x_flat, mod_s, *_mixer_weight_args(lw), h0, lb0, sb0)


def _top_values(work, k, with_rank=False):
    kp = -(-k // SUBLANES) * SUBLANES
    row = lax.broadcasted_iota(jnp.int32, (kp, 1), 0)
    out = jnp.full((kp, work.shape[1]), -jnp.inf, F32)
    rank = jnp.full(work.shape, float(work.shape[0]), F32) if with_rank else None
    for j in range(k):
        m = jnp.max(work, axis=0, keepdims=True)
        out = jnp.where(row == j, m, out)
        hit = work == m
        if with_rank:
            rank = jnp.where(hit, float(j), rank)
        if j + 1 < k:
            work = jnp.where(hit, -jnp.inf, work)
    return (out, rank) if with_rank else out


def _prep_kernel(x1_ref, mod_ref, n2g_ref, wq_ref, sk_ref, h2t_ref, kc_ref, e1_ref, r2_ref, e2_ref):
    nk = sk_ref.shape[1]
    n_heads = wq_ref.shape[1] // (2 * nk)
    k = PEER_TOPK
    shift2, scale2 = mod_ref[3], mod_ref[4]
    h2 = _rms(x1_ref[...], n2g_ref[...]) * (1.0 + scale2) + shift2
    h2t_ref[...] = h2.T.astype(BF16)
    qb = _dot(h2.astype(BF16), wq_ref[...]).astype(BF16)
    nt = (((1,), (1,)), ((), ()))
    for hd in range(n_heads):
        s1, s2 = [lax.dot_general(sk_ref[p], qb[:, (2 * hd + p) * nk:(2 * hd + p + 1) * nk], nt,
                                  preferred_element_type=F32) for p in range(2)]
        v1 = _top_values(s1, k)
        v2, rank2 = _top_values(s2, k, with_rank=True)
        blocks = [v1[0:1, :] + v2]
        for a in range(1, SUBLANES):
            blocks.append(v1[a:a + 1, :] + v2[0:SUBLANES, :])
        blocks.append(v1[SUBLANES:, :] + v2[0:1, :])
        top = _top_values(jnp.concatenate(blocks, axis=0), k)
        z = jnp.sum(jnp.exp(top - top[0:1, :]), axis=0, keepdims=True)
        tau = top[k - 1:k, :]
        count_a = jnp.zeros_like(v1)
        for b in range(k):
            count_a = count_a + jnp.where(v1 + v2[b:b + 1, :] >= tau, 1.0, 0.0)
        count = jnp.zeros_like(s1)
        for a in range(k):
            count = jnp.where(s1 == v1[a:a + 1, :], count_a[a:a + 1, :], count)
        kc_ref[hd] = count
        e1_ref[hd] = jnp.exp(s1 - v1[0:1, :]) * (1.0 / z)
        r2_ref[hd * nk:(hd + 1) * nk, :] = rank2.astype(BF16)
        e2_ref[hd * nk:(hd + 1) * nk, :] = jnp.exp(s2 - v2[0:1, :]).astype(BF16)


def _peer_prep(x1_flat, mod, mod_spec, rows, lw):
    t, d = x1_flat.shape
    nq = lw["peer_wq"].shape[1]
    nk = lw["peer_sub_keys"].shape[1]
    n_heads = nq // (2 * nk)
    return pl.pallas_call(
        _prep_kernel,
        grid=(t // rows,),
        in_specs=[pl.BlockSpec((rows, d), lambda i: (i, 0)), mod_spec,
                  _const_spec(lw["norm2_g"].shape), _const_spec(lw["peer_wq"].shape),
                  _const_spec(lw["peer_sub_keys"].shape)],
        out_specs=[pl.BlockSpec((d, rows), lambda i: (0, i)),
                   pl.BlockSpec((n_heads, nk, rows), lambda i: (0, 0, i)),
                   pl.BlockSpec((n_heads, nk, rows), lambda i: (0, 0, i)),
                   pl.BlockSpec((n_heads * nk, rows), lambda i: (0, i)),
                   pl.BlockSpec((n_heads * nk, rows), lambda i: (0, i))],
        out_shape=[jax.ShapeDtypeStruct((d, t), BF16),
                   jax.ShapeDtypeStruct((n_heads, nk, t), F32),
                   jax.ShapeDtypeStruct((n_heads, nk, t), F32),
                   jax.ShapeDtypeStruct((n_heads * nk, t), BF16),
                   jax.ShapeDtypeStruct((n_heads * nk, t), BF16)],
        compiler_params=_params(("arbitrary",), 48),
    )(x1_flat, mod, lw["norm2_g"], lw["peer_wq"], lw["peer_sub_keys"])


def _peer_kernel(h2t_ref, kc_ref, e1_ref, r2_ref, e2_ref, u_ref, vt_ref, out_ref,
                 bc_ref, s0_ref, s1_ref, a_ref, *, n_heads, nk, n_chunks):
    g = pl.program_id(0)
    c_gate = jnp.maximum(g - 1, 0) % n_chunks
    ec, tb = s0_ref.shape
    i_group = bc_ref.shape[0]
    s_rows = i_group * nk
    jrows = bc_ref.shape[2]
    gl = min(PEER_GATE_LANES, tb)

    @pl.when(g == 0)
    def _():
        s1_ref[...] = jnp.zeros_like(s1_ref)

    @pl.when(c_gate == 0)
    def _():
        out_ref[...] = jnp.zeros_like(out_ref)

    def run(s_w, s_r):
        def gate_piece(r1, jq, l0):
            tiles = [jq * PEER_J_TILES + jt for jt in range(PEER_J_TILES)]
            gates = [[jnp.zeros((jrows, gl), BF16) for _ in tiles] for _ in range(i_group)]
            for hd in range(n_heads):
                kcb = [bc_ref[ii, 2 * hd, :, l0:l0 + gl] for ii in range(i_group)]
                e1b = [bc_ref[ii, 2 * hd + 1, :, l0:l0 + gl] for ii in range(i_group)]
                for n, jt in enumerate(tiles):
                    j0 = hd * nk + jt * jrows
                    r2 = r2_ref[j0:j0 + jrows, l0:l0 + gl]
                    e2 = e2_ref[j0:j0 + jrows, l0:l0 + gl]
                    for ii in range(i_group):
                        gates[ii][n] = gates[ii][n] + jnp.where(r2 < kcb[ii], e2 * e1b[ii], 0.0)
            for ii in range(i_group):
                for n, jt in enumerate(tiles):
                    row = r1 + ii * nk + jt * jrows
                    act = jax.nn.gelu(s_r[row:row + jrows, l0:l0 + gl]) * gates[ii][n]
                    a_ref[row:row + jrows, l0:l0 + gl] = act

        s_w[...] = _dot(u_ref[...], h2t_ref[...]).astype(BF16)
        for it in range(ec // s_rows):
            r1 = it * s_rows
            for ii in range(i_group):
                il = it * i_group + ii
                for hd in range(n_heads):
                    bc_ref[ii, 2 * hd] = jnp.broadcast_to(kc_ref[hd, il:il + 1, :].astype(BF16), (jrows, tb))
                    bc_ref[ii, 2 * hd + 1] = jnp.broadcast_to(e1_ref[hd, il:il + 1, :].astype(BF16), (jrows, tb))
            for jq, l0 in itertools.product(range(nk // (jrows * PEER_J_TILES)), range(0, tb, gl)):
                gate_piece(r1, jq, l0)
            out_ref[...] += _dot(vt_ref[:, r1:r1 + s_rows], a_ref[r1:r1 + s_rows, :])

    @pl.when(g % 2 == 0)
    def _():
        run(s0_ref, s1_ref)

    @pl.when(g % 2 == 1)
    def _():
        run(s1_ref, s0_ref)


def _peer_experts(h2t, kc, e1, r2, e2, u_bf, vt_bf):
    d, t = h2t.shape
    n_heads, nk, _ = kc.shape
    ne = u_bf.shape[0]
    tb = min(PEER_TOKENS, t)
    ec = PEER_EXPERTS
    nc = ne // ec
    n = (t // tb) * nc
    gate_step = lambda g: jnp.maximum(g - 1, 0)
    chunk_tab = pl.BlockSpec((n_heads, ec // nk, tb), lambda g: (0, gate_step(g) % nc, gate_step(g) // nc))
    token_tab = pl.BlockSpec((n_heads * nk, tb), lambda g: (0, gate_step(g) // nc),
                             pipeline_mode=pl.Buffered(1))
    return pl.pallas_call(
        functools.partial(_peer_kernel, n_heads=n_heads, nk=nk, n_chunks=nc),
        grid=(n + 1,),
        in_specs=[pl.BlockSpec((d, tb), lambda g: (0, jnp.minimum(g, n - 1) // nc),
                               pipeline_mode=pl.Buffered(1)),
                  chunk_tab, chunk_tab, token_tab, token_tab,
                  pl.BlockSpec((ec, d), lambda g: (jnp.minimum(g, n - 1) % nc, 0)),
                  pl.BlockSpec((None, d, ec), lambda g: (gate_step(g) % nc, 0, 0))],
        out_specs=pl.BlockSpec((d, tb), lambda g: (0, gate_step(g) // nc)),
        out_shape=jax.ShapeDtypeStruct((d, t), F32),
        scratch_shapes=[pltpu.VMEM((PEER_I_GROUP, 2 * n_heads, 2 * SUBLANES, tb), BF16),
                        pltpu.VMEM((ec, tb), BF16),
                        pltpu.VMEM((ec, tb), BF16),
                        pltpu.VMEM((ec, tb), BF16)],
        compiler_params=_params(("arbitrary",), 52),
    )(h2t, kc, e1, r2, e2, u_bf, vt_bf)


def _epilogue_kernel(pt_ref, x1_ref, mod_ref, fg_ref, y_ref, *, final_norm):
    xo = x1_ref[...] + mod_ref[5] * pt_ref[...].T
    y_ref[...] = _rms(xo, fg_ref[...]) if final_norm else xo


def _epilogue(peer_t, x1_flat, mod, mod_spec, rows, final_g, final_norm):
    t, d = x1_flat.shape
    return pl.pallas_call(
        functools.partial(_epilogue_kernel, final_norm=final_norm),
        grid=(t // rows,),
        in_specs=[pl.BlockSpec((d, rows), lambda i: (0, i)),
                  pl.BlockSpec((rows, d), lambda i: (i, 0)), mod_spec, _const_spec(final_g.shape)],
        out_specs=pl.BlockSpec((rows, d), lambda i: (i, 0)),
        out_shape=jax.ShapeDtypeStruct((t, d), F32),
        compiler_params=_params(("arbitrary",), 40),
    )(peer_t, x1_flat, mod, final_g)


def _block_diag(w):
    heads, hd, _ = w.shape
    per = MXU_DIM // hd
    wg = w.reshape(heads // per, per, hd, hd)
    eye = jnp.eye(per, dtype=w.dtype)
    return jnp.einsum("gpij,pq->gpiqj", wg, eye).reshape(heads // per, MXU_DIM, MXU_DIM).astype(BF16)


def kernel(x_prompt, x_sample, c_prompt, c_sample, state_lru_h, state_lru_conv, state_sconv, w_ada, b_ada, norm1_g, norm2_g, w_in, lru_conv_w, lru_conv_b, lru_wa, lru_ba, lru_wx, lru_bx, lru_lambda, sconv_w, gnorm_lru_g, gnorm_sc_g, w_out, peer_wq, peer_sub_keys, peer_u, peer_v, final_g):
    bp, seq, d = x_prompt.shape
    bs, sseq, _ = x_sample.shape
    depth = w_ada.shape[0]
    nk = peer_sub_keys.shape[2]
    n_heads = peer_wq.shape[2] // (2 * nk)
    tps = seq // PROMPT_TILE

    xp = x_prompt.reshape(bp * seq, d)
    xs = jnp.swapaxes(x_sample, 0, 1).reshape(sseq * bs, d)
    n_c = bp + bs
    pad = (-n_c) % SUBLANES
    c_all = jnp.concatenate([c_prompt, c_sample, jnp.zeros((pad, d), F32)], axis=0)
    fg = final_g.reshape(1, d)

    mod_p_spec = pl.BlockSpec((6, None, 1, d), lambda i: (0, i // tps, 0, 0))
    mod_s_spec = _const_spec((6, bs, d))

    outs = [[] for _ in range(6)]
    for l in range(depth):
        row = lambda a: a[l].reshape(1, -1)
        lw = dict(
            norm1_g=row(norm1_g), norm2_g=row(norm2_g), w_in=w_in[l].astype(BF16),
            lru_conv_w=lru_conv_w[l], lru_conv_b=row(lru_conv_b),
            wa_bd=_block_diag(lru_wa[l]), lru_ba=row(lru_ba),
            wx_bd=_block_diag(lru_wx[l]), lru_bx=row(lru_bx), lru_lambda=row(lru_lambda),
            sconv_w=sconv_w[l], gnorm_lru_g=row(gnorm_lru_g), gnorm_sc_g=row(gnorm_sc_g),
            w_out=w_out[l].astype(BF16), peer_wq=peer_wq[l].astype(BF16),
            peer_sub_keys=peer_sub_keys[l].astype(BF16))
        u_bf = peer_u[l].astype(BF16)
        ne = peer_v.shape[1]
        vt_bf = jnp.swapaxes(peer_v[l].reshape(ne // PEER_EXPERTS, PEER_EXPERTS, d), 1, 2).astype(BF16)

        mod = _adaln(c_all, w_ada[l], b_ada[l])
        mod_p = jnp.swapaxes(mod[:bp].reshape(bp, 6, d), 0, 1).reshape(6, bp, 1, d)
        mod_s = jnp.swapaxes(mod[bp:n_c].reshape(bs, 6, d), 0, 1)

        xp1, hl, lb, sb = _mixer_prompt(xp, mod_p, lw, bp, seq)
        h0 = state_lru_h[l]
        lb0 = jnp.swapaxes(state_lru_conv[l], 0, 1)
        sb0 = jnp.swapaxes(state_sconv[l], 0, 1)
        xs1, hs, lbs, sbs = _mixer_sample(xs, mod_s, lw, h0, lb0, sb0, bs)

        last = l == depth - 1
        new_x = []
        for x1, mod_g, spec, rows in ((xp1, mod_p, mod_p_spec, PROMPT_TILE), (xs1, mod_s, mod_s_spec, bs)):
            peer_t = _peer_experts(*_peer_prep(x1, mod_g, spec, rows, lw), u_bf, vt_bf)
            new_x.append(_epilogue(peer_t, x1, mod_g, spec, rows, fg, last))
        xp, xs = new_x

        nl = lru_conv_w.shape[1] - 1
        ns = sconv_w.shape[1] - 1
        outs[0].append(hl[:, SUBLANES - 1, :])
        outs[1].append(lb[:, SUBLANES - nl:, :])
        outs[2].append(sb[:, SUBLANES - ns:, :])
        outs[3].append(hs)
        outs[4].append(jnp.swapaxes(lbs, 0, 1))
        outs[5].append(jnp.swapaxes(sbs, 0, 1))

    y_prompt = xp.reshape(bp, seq, d)
    y_sample = jnp.swapaxes(xs.reshape(sseq, bs, d), 0, 1)
    return (y_prompt, y_sample) + tuple(jnp.stack(o) for o in outs)
```

```python
import functools
import itertools

import jax
import jax.numpy as jnp
from jax import lax
from jax.experimental import pallas as pl
from jax.experimental.pallas import tpu as pltpu

F32 = jnp.float32
BF16 = jnp.bfloat16

EPS = 1e-6
LRU_C = 8.0
PEER_TOPK = 16

LANES = 128
SUBLANES = 8
MXU_DIM = 256
VMEM_BYTES = 64 * 1024 * 1024

PROMPT_TILE = 256
PEER_TOKENS = 512
PEER_EXPERTS = 1024
PEER_I_GROUP = 2
PEER_GATE_LANES = 256
PEER_J_TILES = 2
ADALN_COLS = 1024


def _dot(a, b):
    return jnp.dot(a, b, preferred_element_type=F32)


def _rms(x, g):
    ms = jnp.mean(x * x, axis=-1, keepdims=True)
    return x * lax.rsqrt(ms + EPS) * g


def _const_spec(shape):
    nd = len(shape)
    return pl.BlockSpec(shape, lambda *_: (0,) * nd, pipeline_mode=pl.Buffered(1))


def _params(sem, vmem_mb):
    return pltpu.CompilerParams(dimension_semantics=sem, vmem_limit_bytes=vmem_mb * 1024 * 1024)


def _adaln_kernel(c_ref, w_ref, b_ref, o_ref):
    sc = jax.nn.silu(c_ref[...]).astype(BF16)
    o_ref[...] = _dot(sc, w_ref[...].astype(BF16)) + b_ref[...]


def _adaln(c_all, w_ada, b_ada):
    m, d = c_all.shape
    n = w_ada.shape[1]
    tn = ADALN_COLS
    return pl.pallas_call(
        _adaln_kernel,
        grid=(n // tn,),
        in_specs=[pl.BlockSpec((m, d), lambda j: (0, 0)),
                  pl.BlockSpec((d, tn), lambda j: (0, j)),
                  pl.BlockSpec((1, tn), lambda j: (0, j))],
        out_specs=pl.BlockSpec((m, tn), lambda j: (0, j)),
        out_shape=jax.ShapeDtypeStruct((m, n), F32),
        compiler_params=_params(("arbitrary",), 40),
    )(c_all, w_ada, b_ada.reshape(1, n))


def _gate_pre(xcb, w_ref, b_ref):
    nblk = w_ref.shape[0]
    parts = [_dot(xcb[:, g * MXU_DIM:(g + 1) * MXU_DIM], w_ref[g]) for g in range(nblk)]
    return jnp.concatenate(parts, axis=1) + b_ref[...]


def _scan_rows(a, u, h_in):
    rows = a.shape[0]
    row = lax.broadcasted_iota(jnp.int32, (rows, 1), 0)
    acc_a = a
    acc_b = u + jnp.where(row == 0, a * h_in, 0.0)
    d = 1
    while d < rows:
        keep = row >= d
        sh_b = pltpu.roll(acc_b, d, axis=0)
        acc_b = acc_b + jnp.where(keep, acc_a * sh_b, 0.0)
        if 2 * d < rows:
            sh_a = pltpu.roll(acc_a, d, axis=0)
            acc_a = jnp.where(keep, acc_a * sh_a, acc_a)
        d *= 2
    return acc_b


def _mixer_core(x, mod_ref, n1g_ref, win_ref, cb_ref, wa_ref, ba_ref, wx_ref, bx_ref, lam_ref,
                gl_ref, gs_ref, wout_ref, lru_conv, lru_scan, sc_conv):
    w = cb_ref.shape[-1]
    v = gs_ref.shape[-1]
    shift1, scale1, gate1 = mod_ref[0], mod_ref[1], mod_ref[2]
    h = _rms(x, n1g_ref[...]) * (1.0 + scale1) + shift1
    hb = h.astype(BF16)

    x_lru = _dot(hb, win_ref[:, 0:w])
    xc = lru_conv(x_lru) + cb_ref[...]
    xcb = xc.astype(BF16)
    r = jax.nn.sigmoid(_gate_pre(xcb, wa_ref, ba_ref))
    i = jax.nn.sigmoid(_gate_pre(xcb, wx_ref, bx_ref))
    log_a = (-LRU_C) * r * jax.nn.softplus(-lam_ref[...])
    a = jnp.exp(log_a)
    u = jnp.sqrt(jnp.tanh(-log_a) * (1.0 + a * a)) * (i * xc)
    rec = lru_scan(a, u)
    y_gate = _dot(hb, win_ref[:, w:2 * w])
    out_lru = rec * jax.nn.gelu(y_gate)

    sc_b = _dot(hb, win_ref[:, 2 * w:2 * w + v])
    sc_c = _dot(hb, win_ref[:, 2 * w + v:2 * w + 2 * v])
    sc_x = _dot(hb, win_ref[:, 2 * w + 2 * v:2 * w + 3 * v])
    cx = sc_c * sc_x
    out_sc = sc_b * sc_conv(cx)

    n_l = _rms(out_lru, gl_ref[...]).astype(BF16)
    n_s = _rms(out_sc, gs_ref[...]).astype(BF16)
    mix = _dot(n_l, wout_ref[0:w, :]) + _dot(n_s, wout_ref[w:w + v, :])
    return x + gate1 * mix, x_lru, rec, cx


def _mixer_prompt_kernel(x_ref, mod_ref, n1g_ref, win_ref, cw_ref, cb_ref, wa_ref, ba_ref, wx_ref,
                         bx_ref, lam_ref, sw_ref, gl_ref, gs_ref, wout_ref,
                         x1_ref, hl_ref, lb_ref, sb_ref, *, tiles_per_seq):
    rows = x_ref.shape[0]
    hdr = SUBLANES

    @pl.when(pl.program_id(0) % tiles_per_seq == 0)
    def _():
        hl_ref[...] = jnp.zeros_like(hl_ref)
        lb_ref[...] = jnp.zeros_like(lb_ref)
        sb_ref[...] = jnp.zeros_like(sb_ref)

    def conv(cur, tail_ref, w_ref):
        width = w_ref.shape[0]
        ext = jnp.concatenate([tail_ref[...], cur], axis=0)
        out = w_ref[width - 1:width, :] * cur
        for d in range(1, width):
            out = out + w_ref[width - 1 - d:width - d, :] * pltpu.roll(ext, d, axis=0)[hdr:]
        return out

    new_x, x_lru, rec, cx = _mixer_core(
        x_ref[...], mod_ref, n1g_ref, win_ref, cb_ref, wa_ref, ba_ref, wx_ref, bx_ref, lam_ref,
        gl_ref, gs_ref, wout_ref,
        lru_conv=lambda cur: conv(cur, lb_ref, cw_ref),
        lru_scan=lambda a, u: _scan_rows(a, u, hl_ref[hdr - 1:hdr, :]),
        sc_conv=lambda cur: conv(cur, sb_ref, sw_ref))
    x1_ref[...] = new_x
    hl_ref[...] = rec[rows - hdr:, :]
    lb_ref[...] = x_lru[rows - hdr:, :]
    sb_ref[...] = cx[rows - hdr:, :]


def _mixer_sample_kernel(x_ref, mod_ref, n1g_ref, win_ref, cw_ref, cb_ref, wa_ref, ba_ref, wx_ref,
                         bx_ref, lam_ref, sw_ref, gl_ref, gs_ref, wout_ref, h0_ref, lb0_ref, sb0_ref,
                         x1_ref, h_ref, lb_ref, sb_ref):
    @pl.when(pl.program_id(0) == 0)
    def _():
        h_ref[...] = h0_ref[...]
        lb_ref[...] = lb0_ref[...]
        sb_ref[...] = sb0_ref[...]

    def conv(cur, hist_ref, w_ref):
        width = w_ref.shape[0]
        out = w_ref[width - 1:width, :] * cur
        for k in range(width - 1):
            out = out + w_ref[k:k + 1, :] * hist_ref[k]
        for k in range(width - 2):
            hist_ref[k] = hist_ref[k + 1]
        hist_ref[width - 2] = cur
        return out

    def scan(a, u):
        h = a * h_ref[...] + u
        h_ref[...] = h
        return h

    new_x, _, _, _ = _mixer_core(
        x_ref[...], mod_ref, n1g_ref, win_ref, cb_ref, wa_ref, ba_ref, wx_ref, bx_ref, lam_ref,
        gl_ref, gs_ref, wout_ref,
        lru_conv=lambda cur: conv(cur, lb_ref, cw_ref),
        lru_scan=scan,
        sc_conv=lambda cur: conv(cur, sb_ref, sw_ref))
    x1_ref[...] = new_x


def _mixer_weight_specs(lw):
    return [_const_spec(lw[k].shape) for k in
            ("norm1_g", "w_in", "lru_conv_w", "lru_conv_b", "wa_bd", "lru_ba", "wx_bd", "lru_bx",
             "lru_lambda", "sconv_w", "gnorm_lru_g", "gnorm_sc_g", "w_out")]


def _mixer_weight_args(lw):
    return [lw[k] for k in
            ("norm1_g", "w_in", "lru_conv_w", "lru_conv_b", "wa_bd", "lru_ba", "wx_bd", "lru_bx",
             "lru_lambda", "sconv_w", "gnorm_lru_g", "gnorm_sc_g", "w_out")]


def _mixer_prompt(x_flat, mod_p, lw, batch, seq):
    t, d = x_flat.shape
    w = lw["lru_conv_b"].shape[-1]
    v = lw["gnorm_sc_g"].shape[-1]
    r = PROMPT_TILE
    tps = seq // r
    state = lambda c: pl.BlockSpec((None, SUBLANES, c), lambda i: (i // tps, 0, 0))
    return pl.pallas_call(
        functools.partial(_mixer_prompt_kernel, tiles_per_seq=tps),
        grid=(t // r,),
        in_specs=[pl.BlockSpec((r, d), lambda i: (i, 0)),
                  pl.BlockSpec((6, None, 1, d), lambda i: (0, i // tps, 0, 0))] + _mixer_weight_specs(lw),
        out_specs=[pl.BlockSpec((r, d), lambda i: (i, 0)), state(w), state(w), state(v)],
        out_shape=[jax.ShapeDtypeStruct((t, d), F32),
                   jax.ShapeDtypeStruct((batch, SUBLANES, w), F32),
                   jax.ShapeDtypeStruct((batch, SUBLANES, w), F32),
                   jax.ShapeDtypeStruct((batch, SUBLANES, v), F32)],
        compiler_params=_params(("arbitrary",), 60),
    )(x_flat, mod_p, *_mixer_weight_args(lw))


def _mixer_sample(x_flat, mod_s, lw, h0, lb0, sb0, batch):
    t, d = x_flat.shape
    w = lw["lru_conv_b"].shape[-1]
    v = lw["gnorm_sc_g"].shape[-1]
    full = lambda shape: pl.BlockSpec(shape, lambda i: (0,) * len(shape))
    return pl.pallas_call(
        _mixer_sample_kernel,
        grid=(t // batch,),
        in_specs=[pl.BlockSpec((batch, d), lambda i: (i, 0)), _const_spec(mod_s.shape)]
        + _mixer_weight_specs(lw) + [_const_spec(h0.shape), _const_spec(lb0.shape), _const_spec(sb0.shape)],
        out_specs=[pl.BlockSpec((batch, d), lambda i: (i, 0)), full(h0.shape), full(lb0.shape), full(sb0.shape)],
        out_shape=[jax.ShapeDtypeStruct((t, d), F32),
                   jax.ShapeDtypeStruct(h0.shape, F32),
                   jax.ShapeDtypeStruct(lb0.shape, F32),
                   jax.ShapeDtypeStruct(sb0.shape, F32)],
        compiler_params=_params(("arbitrary",), 60),
    )(x_flat, mod_s, *_mixer_weight_args(lw), h0, lb0, sb0)


def _top_values(work, k, with_rank=False):
    kp = -(-k // SUBLANES) * SUBLANES
    row = lax.broadcasted_iota(jnp.int32, (kp, 1), 0)
    out = jnp.full((kp, work.shape[1]), -jnp.inf, F32)
    rank = jnp.full(work.shape, float(work.shape[0]), F32) if with_rank else None
    for j in range(k):
        m = jnp.max(work, axis=0, keepdims=True)
        out = jnp.where(row == j, m, out)
        hit = work == m
        if with_rank:
            rank = jnp.where(hit, float(j), rank)
        if j + 1 < k:
            work = jnp.where(hit, -jnp.inf, work)
    return (out, rank) if with_rank else out


def _prep_kernel(x1_ref, mod_ref, n2g_ref, wq_ref, sk_ref, h2t_ref, kc_ref, e1_ref, r2_ref, e2_ref):
    nk = sk_ref.shape[1]
    n_heads = wq_ref.shape[1] // (2 * nk)
    k = PEER_TOPK
    shift2, scale2 = mod_ref[3], mod_ref[4]
    h2 = _rms(x1_ref[...], n2g_ref[...]) * (1.0 + scale2) + shift2
    h2t_ref[...] = h2.T.astype(BF16)
    qb = _dot(h2.astype(BF16), wq_ref[...]).astype(BF16)
    nt = (((1,), (1,)), ((), ()))
    for hd in range(n_heads):
        s1, s2 = [lax.dot_general(sk_ref[p], qb[:, (2 * hd + p) * nk:(2 * hd + p + 1) * nk], nt,
                                  preferred_element_type=F32) for p in range(2)]
        v1 = _top_values(s1, k)
        v2, rank2 = _top_values(s2, k, with_rank=True)
        blocks = [v1[0:1, :] + v2]
        for a in range(1, SUBLANES):
            blocks.append(v1[a:a + 1, :] + v2[0:SUBLANES, :])
        blocks.append(v1[SUBLANES:, :] + v2[0:1, :])
        top = _top_values(jnp.concatenate(blocks, axis=0), k)
        z = jnp.sum(jnp.exp(top - top[0:1, :]), axis=0, keepdims=True)
        tau = top[k - 1:k, :]
        count_a = jnp.zeros_like(v1)
        for b in range(k):
            count_a = count_a + jnp.where(v1 + v2[b:b + 1, :] >= tau, 1.0, 0.0)
        count = jnp.zeros_like(s1)
        for a in range(k):
            count = jnp.where(s1 == v1[a:a + 1, :], count_a[a:a + 1, :], count)
        kc_ref[hd] = count
        e1_ref[hd] = jnp.exp(s1 - v1[0:1, :]) * (1.0 / z)
        r2_ref[hd * nk:(hd + 1) * nk, :] = rank2.astype(BF16)
        e2_ref[hd * nk:(hd + 1) * nk, :] = jnp.exp(s2 - v2[0:1, :]).astype(BF16)


def _peer_prep(x1_flat, mod, mod_spec, rows, lw):
    t, d = x1_flat.shape
    nq = lw["peer_wq"].shape[1]
    nk = lw["peer_sub_keys"].shape[1]
    n_heads = nq // (2 * nk)
    return pl.pallas_call(
        _prep_kernel,
        grid=(t // rows,),
        in_specs=[pl.BlockSpec((rows, d), lambda i: (i, 0)), mod_spec,
                  _const_spec(lw["norm2_g"].shape), _const_spec(lw["peer_wq"].shape),
                  _const_spec(lw["peer_sub_keys"].shape)],
        out_specs=[pl.BlockSpec((d, rows), lambda i: (0, i)),
                   pl.BlockSpec((n_heads, nk, rows), lambda i: (0, 0, i)),
                   pl.BlockSpec((n_heads, nk, rows), lambda i: (0, 0, i)),
                   pl.BlockSpec((n_heads * nk, rows), lambda i: (0, i)),
                   pl.BlockSpec((n_heads * nk, rows), lambda i: (0, i))],
        out_shape=[jax.ShapeDtypeStruct((d, t), BF16),
                   jax.ShapeDtypeStruct((n_heads, nk, t), F32),
                   jax.ShapeDtypeStruct((n_heads, nk, t), F32),
                   jax.ShapeDtypeStruct((n_heads * nk, t), BF16),
                   jax.ShapeDtypeStruct((n_heads * nk, t), BF16)],
        compiler_params=_params(("arbitrary",), 48),
    )(x1_flat, mod, lw["norm2_g"], lw["peer_wq"], lw["peer_sub_keys"])


def _peer_kernel(h2t_ref, kc_ref, e1_ref, r2_ref, e2_ref, u_ref, vt_ref, x1_ref, g2_ref, fg_ref, y_ref,
                 bc_ref, s0_ref, s1_ref, a_ref, out_ref, *, n_heads, nk, n_chunks, final_norm):
    g = pl.program_id(0)
    c_gate = jnp.maximum(g - 1, 0) % n_chunks
    ec, tb = s0_ref.shape
    i_group = bc_ref.shape[0]
    s_rows = i_group * nk
    jrows = bc_ref.shape[2]
    gl = min(PEER_GATE_LANES, tb)

    @pl.when(g == 0)
    def _():
        s1_ref[...] = jnp.zeros_like(s1_ref)

    @pl.when(c_gate == 0)
    def _():
        out_ref[...] = jnp.zeros_like(out_ref)

    def run(s_w, s_r):
        def gate_piece(r1, jq, l0):
            tiles = [jq * PEER_J_TILES + jt for jt in range(PEER_J_TILES)]
            gates = [[jnp.zeros((jrows, gl), BF16) for _ in tiles] for _ in range(i_group)]
            for hd in range(n_heads):
                kcb = [bc_ref[ii, 2 * hd, :, l0:l0 + gl] for ii in range(i_group)]
                e1b = [bc_ref[ii, 2 * hd + 1, :, l0:l0 + gl] for ii in range(i_group)]
                for n, jt in enumerate(tiles):
                    j0 = hd * nk + jt * jrows
                    r2 = r2_ref[j0:j0 + jrows, l0:l0 + gl]
                    e2 = e2_ref[j0:j0 + jrows, l0:l0 + gl]
                    for ii in range(i_group):
                        gates[ii][n] = gates[ii][n] + jnp.where(r2 < kcb[ii], e2 * e1b[ii], 0.0)
            for ii in range(i_group):
                for n, jt in enumerate(tiles):
                    row = r1 + ii * nk + jt * jrows
                    act = jax.nn.gelu(s_r[row:row + jrows, l0:l0 + gl]) * gates[ii][n]
                    a_ref[row:row + jrows, l0:l0 + gl] = act

        s_w[...] = _dot(u_ref[...], h2t_ref[...]).astype(BF16)
        for it in range(ec // s_rows):
            r1 = it * s_rows
            for ii in range(i_group):
                il = it * i_group + ii
                for hd in range(n_heads):
                    bc_ref[ii, 2 * hd] = jnp.broadcast_to(kc_ref[hd, il:il + 1, :].astype(BF16), (jrows, tb))
                    bc_ref[ii, 2 * hd + 1] = jnp.broadcast_to(e1_ref[hd, il:il + 1, :].astype(BF16), (jrows, tb))
            for jq, l0 in itertools.product(range(nk // (jrows * PEER_J_TILES)), range(0, tb, gl)):
                gate_piece(r1, jq, l0)
            out_ref[...] += _dot(vt_ref[:, r1:r1 + s_rows], a_ref[r1:r1 + s_rows, :])

    @pl.when(g % 2 == 0)
    def _():
        run(s0_ref, s1_ref)

    @pl.when(g % 2 == 1)
    def _():
        run(s1_ref, s0_ref)

    @pl.when((g > 0) & (c_gate == n_chunks - 1))
    def _():
        rows = g2_ref.shape[0]
        r_n = min(max(rows, MXU_DIM), tb)
        for r0 in range(0, tb, r_n):
            peer = out_ref[:, r0:r0 + r_n].T
            gate2 = g2_ref[...] if rows in (1, r_n) else jnp.concatenate([g2_ref[...]] * (r_n // rows), axis=0)
            xo = x1_ref[r0:r0 + r_n, :] + gate2 * peer
            y_ref[r0:r0 + r_n, :] = _rms(xo, fg_ref[...]) if final_norm else xo


def _peer_experts(h2t, kc, e1, r2, e2, u_bf, vt_bf, x1_flat, gate2, gate2_spec, final_g, final_norm):
    d, t = h2t.shape
    n_heads, nk, _ = kc.shape
    ne = u_bf.shape[0]
    tb = min(PEER_TOKENS, t)
    ec = PEER_EXPERTS
    nc = ne // ec
    n = (t // tb) * nc
    gate_step = lambda g: jnp.maximum(g - 1, 0)
    chunk_tab = pl.BlockSpec((n_heads, ec // nk, tb), lambda g: (0, gate_step(g) % nc, gate_step(g) // nc))
    token_tab = pl.BlockSpec((n_heads * nk, tb), lambda g: (0, gate_step(g) // nc),
                             pipeline_mode=pl.Buffered(1))
    return pl.pallas_call(
        functools.partial(_peer_kernel, n_heads=n_heads, nk=nk, n_chunks=nc, final_norm=final_norm),
        grid=(n + 1,),
        in_specs=[pl.BlockSpec((d, tb), lambda g: (0, jnp.minimum(g, n - 1) // nc),
                               pipeline_mode=pl.Buffered(1)),
                  chunk_tab, chunk_tab, token_tab, token_tab,
                  pl.BlockSpec((ec, d), lambda g: (jnp.minimum(g, n - 1) % nc, 0)),
                  pl.BlockSpec((None, d, ec), lambda g: (gate_step(g) % nc, 0, 0)),
                  pl.BlockSpec((tb, d), lambda g: (gate_step(g) // nc, 0), pipeline_mode=pl.Buffered(1)),
                  gate2_spec(lambda g: gate_step(g) // nc), _const_spec(final_g.shape)],
        out_specs=pl.BlockSpec((tb, d), lambda g: (gate_step(g) // nc, 0)),
        out_shape=jax.ShapeDtypeStruct((t, d), F32),
        scratch_shapes=[pltpu.VMEM((PEER_I_GROUP, 2 * n_heads, 2 * SUBLANES, tb), BF16),
                        pltpu.VMEM((ec, tb), BF16),
                        pltpu.VMEM((ec, tb), BF16),
                        pltpu.VMEM((ec, tb), BF16),
                        pltpu.VMEM((d, tb), F32)],
        compiler_params=_params(("arbitrary",), 56),
    )(h2t, kc, e1, r2, e2, u_bf, vt_bf, x1_flat, gate2, final_g)


def _block_diag(w):
    heads, hd, _ = w.shape
    per = MXU_DIM // hd
    wg = w.reshape(heads // per, per, hd, hd)
    eye = jnp.eye(per, dtype=w.dtype)
    return jnp.einsum("gpij,pq->gpiqj", wg, eye).reshape(heads // per, MXU_DIM, MXU_DIM).astype(BF16)


def kernel(x_prompt, x_sample, c_prompt, c_sample, state_lru_h, state_lru_conv, state_sconv, w_ada, b_ada, norm1_g, norm2_g, w_in, lru_conv_w, lru_conv_b, lru_wa, lru_ba, lru_wx, lru_bx, lru_lambda, sconv_w, gnorm_lru_g, gnorm_sc_g, w_out, peer_wq, peer_sub_keys, peer_u, peer_v, final_g):
    bp, seq, d = x_prompt.shape
    bs, sseq, _ = x_sample.shape
    depth = w_ada.shape[0]
    nk = peer_sub_keys.shape[2]
    n_heads = peer_wq.shape[2] // (2 * nk)
    tps = seq // PROMPT_TILE

    xp = x_prompt.reshape(bp * seq, d)
    xs = jnp.swapaxes(x_sample, 0, 1).reshape(sseq * bs, d)
    n_c = bp + bs
    pad = (-n_c) % SUBLANES
    c_all = jnp.concatenate([c_prompt, c_sample, jnp.zeros((pad, d), F32)], axis=0)
    fg = final_g.reshape(1, d)

    mod_p_spec = pl.BlockSpec((6, None, 1, d), lambda i: (0, i // tps, 0, 0))
    mod_s_spec = _const_spec((6, bs, d))
    blocks_per_seq = seq // PEER_TOKENS
    gate2_p_spec = lambda blk: pl.BlockSpec((None, None, 1, d), lambda g: (5, blk(g) // blocks_per_seq, 0, 0))
    gate2_s_spec = lambda blk: pl.BlockSpec((None, bs, d), lambda g: (5, 0, 0), pipeline_mode=pl.Buffered(1))

    outs = [[] for _ in range(6)]
    for l in range(depth):
        row = lambda a: a[l].reshape(1, -1)
        lw = dict(
            norm1_g=row(norm1_g), norm2_g=row(norm2_g), w_in=w_in[l].astype(BF16),
            lru_conv_w=lru_conv_w[l], lru_conv_b=row(lru_conv_b),
            wa_bd=_block_diag(lru_wa[l]), lru_ba=row(lru_ba),
            wx_bd=_block_diag(lru_wx[l]), lru_bx=row(lru_bx), lru_lambda=row(lru_lambda),
            sconv_w=sconv_w[l], gnorm_lru_g=row(gnorm_lru_g), gnorm_sc_g=row(gnorm_sc_g),
            w_out=w_out[l].astype(BF16), peer_wq=peer_wq[l].astype(BF16),
            peer_sub_keys=peer_sub_keys[l].astype(BF16))
        u_bf = peer_u[l].astype(BF16)
        ne = peer_v.shape[1]
        vt_bf = jnp.swapaxes(peer_v[l].reshape(ne // PEER_EXPERTS, PEER_EXPERTS, d), 1, 2).astype(BF16)

        mod = _adaln(c_all, w_ada[l], b_ada[l])
        mod_p = jnp.swapaxes(mod[:bp].reshape(bp, 6, d), 0, 1).reshape(6, bp, 1, d)
        mod_s = jnp.swapaxes(mod[bp:n_c].reshape(bs, 6, d), 0, 1)

        xp1, hl, lb, sb = _mixer_prompt(xp, mod_p, lw, bp, seq)
        h0 = state_lru_h[l]
        lb0 = jnp.swapaxes(state_lru_conv[l], 0, 1)
        sb0 = jnp.swapaxes(state_sconv[l], 0, 1)
        xs1, hs, lbs, sbs = _mixer_sample(xs, mod_s, lw, h0, lb0, sb0, bs)

        last = l == depth - 1
        new_x = []
        for x1, mod_g, spec, rows, g2_spec in ((xp1, mod_p, mod_p_spec, PROMPT_TILE, gate2_p_spec),
                                               (xs1, mod_s, mod_s_spec, bs, gate2_s_spec)):
            new_x.append(_peer_experts(*_peer_prep(x1, mod_g, spec, rows, lw), u_bf, vt_bf,
                                       x1, mod_g, g2_spec, fg, last))
        xp, xs = new_x

        nl = lru_conv_w.shape[1] - 1
        ns = sconv_w.shape[1] - 1
        outs[0].append(hl[:, SUBLANES - 1, :])
        outs[1].append(lb[:, SUBLANES - nl:, :])
        outs[2].append(sb[:, SUBLANES - ns:, :])
        outs[3].append(hs)
        outs[4].append(jnp.swapaxes(lbs, 0, 1))
        outs[5].append(jnp.swapaxes(sbs, 0, 1))

    y_prompt = xp.reshape(bp, seq, d)
    y_sample = jnp.swapaxes(xs.reshape(sseq, bs, d), 0, 1)
    return (y_prompt, y_sample) + tuple(jnp.stack(o) for o in outs)
```

```python
import functools
import itertools

import jax
import jax.numpy as jnp
from jax import lax
from jax.experimental import pallas as pl
from jax.experimental.pallas import tpu as pltpu

F32 = jnp.float32
BF16 = jnp.bfloat16

EPS = 1e-6
LRU_C = 8.0
PEER_TOPK = 16

LANES = 128
SUBLANES = 8
MXU_DIM = 256
VMEM_BYTES = 64 * 1024 * 1024

PROMPT_TILE = 256
PEER_TOKENS = 512
PEER_EXPERTS = 1024
PEER_I_GROUP = 2
PEER_GATE_LANES = 256
PEER_J_TILES = 2
ADALN_COLS = 1024


def _dot(a, b):
    return jnp.dot(a, b, preferred_element_type=F32)


def _rms(x, g):
    ms = jnp.mean(x * x, axis=-1, keepdims=True)
    return x * lax.rsqrt(ms + EPS) * g


def _const_spec(shape):
    nd = len(shape)
    return pl.BlockSpec(shape, lambda *_: (0,) * nd, pipeline_mode=pl.Buffered(1))


def _params(sem, vmem_mb):
    return pltpu.CompilerParams(dimension_semantics=sem, vmem_limit_bytes=vmem_mb * 1024 * 1024)


def _adaln_kernel(c_ref, w_ref, b_ref, o_ref):
    sc = jax.nn.silu(c_ref[...]).astype(BF16)
    o_ref[...] = _dot(sc, w_ref[...].astype(BF16)) + b_ref[...]


def _adaln(c_all, w_ada, b_ada):
    m, d = c_all.shape
    n = w_ada.shape[1]
    tn = ADALN_COLS
    return pl.pallas_call(
        _adaln_kernel,
        grid=(n // tn,),
        in_specs=[pl.BlockSpec((m, d), lambda j: (0, 0)),
                  pl.BlockSpec((d, tn), lambda j: (0, j)),
                  pl.BlockSpec((1, tn), lambda j: (0, j))],
        out_specs=pl.BlockSpec((m, tn), lambda j: (0, j)),
        out_shape=jax.ShapeDtypeStruct((m, n), F32),
        compiler_params=_params(("arbitrary",), 40),
    )(c_all, w_ada, b_ada.reshape(1, n))


def _gate_pre(xcb, w_ref, b_ref):
    nblk = w_ref.shape[0]
    parts = [_dot(xcb[:, g * MXU_DIM:(g + 1) * MXU_DIM], w_ref[g]) for g in range(nblk)]
    return jnp.concatenate(parts, axis=1) + b_ref[...]


def _scan_rows(a, u, h_in):
    rows = a.shape[0]
    row = lax.broadcasted_iota(jnp.int32, (rows, 1), 0)
    acc_a = a
    acc_b = u + jnp.where(row == 0, a * h_in, 0.0)
    d = 1
    while d < rows:
        keep = row >= d
        sh_b = pltpu.roll(acc_b, d, axis=0)
        acc_b = acc_b + jnp.where(keep, acc_a * sh_b, 0.0)
        if 2 * d < rows:
            sh_a = pltpu.roll(acc_a, d, axis=0)
            acc_a = jnp.where(keep, acc_a * sh_a, acc_a)
        d *= 2
    return acc_b


def _mixer_core(x, mod_ref, n1g_ref, win_ref, cb_ref, wa_ref, ba_ref, wx_ref, bx_ref, lam_ref,
                gl_ref, gs_ref, wout_ref, lru_conv, lru_scan, sc_conv):
    w = cb_ref.shape[-1]
    v = gs_ref.shape[-1]
    shift1, scale1, gate1 = mod_ref[0], mod_ref[1], mod_ref[2]
    h = _rms(x, n1g_ref[...]) * (1.0 + scale1) + shift1
    hb = h.astype(BF16)

    x_lru = _dot(hb, win_ref[:, 0:w])
    xc = lru_conv(x_lru) + cb_ref[...]
    xcb = xc.astype(BF16)
    r = jax.nn.sigmoid(_gate_pre(xcb, wa_ref, ba_ref))
    i = jax.nn.sigmoid(_gate_pre(xcb, wx_ref, bx_ref))
    log_a = (-LRU_C) * r * jax.nn.softplus(-lam_ref[...])
    a = jnp.exp(log_a)
    u = jnp.sqrt(jnp.tanh(-log_a) * (1.0 + a * a)) * (i * xc)
    rec = lru_scan(a, u)
    y_gate = _dot(hb, win_ref[:, w:2 * w])
    out_lru = rec * jax.nn.gelu(y_gate)

    sc_b = _dot(hb, win_ref[:, 2 * w:2 * w + v])
    sc_c = _dot(hb, win_ref[:, 2 * w + v:2 * w + 2 * v])
    sc_x = _dot(hb, win_ref[:, 2 * w + 2 * v:2 * w + 3 * v])
    cx = sc_c * sc_x
    out_sc = sc_b * sc_conv(cx)

    n_l = _rms(out_lru, gl_ref[...]).astype(BF16)
    n_s = _rms(out_sc, gs_ref[...]).astype(BF16)
    mix = _dot(n_l, wout_ref[0:w, :]) + _dot(n_s, wout_ref[w:w + v, :])
    return x + gate1 * mix, x_lru, rec, cx


def _mixer_prompt_kernel(x_ref, mod_ref, n1g_ref, win_ref, cw_ref, cb_ref, wa_ref, ba_ref, wx_ref,
                         bx_ref, lam_ref, sw_ref, gl_ref, gs_ref, wout_ref,
                         x1_ref, hl_ref, lb_ref, sb_ref, *, tiles_per_seq):
    rows = x_ref.shape[0]
    hdr = SUBLANES

    @pl.when(pl.program_id(0) % tiles_per_seq == 0)
    def _():
        hl_ref[...] = jnp.zeros_like(hl_ref)
        lb_ref[...] = jnp.zeros_like(lb_ref)
        sb_ref[...] = jnp.zeros_like(sb_ref)

    def conv(cur, tail_ref, w_ref):
        width = w_ref.shape[0]
        ext = jnp.concatenate([tail_ref[...], cur], axis=0)
        out = w_ref[width - 1:width, :] * cur
        for d in range(1, width):
            out = out + w_ref[width - 1 - d:width - d, :] * pltpu.roll(ext, d, axis=0)[hdr:]
        return out

    new_x, x_lru, rec, cx = _mixer_core(
        x_ref[...], mod_ref, n1g_ref, win_ref, cb_ref, wa_ref, ba_ref, wx_ref, bx_ref, lam_ref,
        gl_ref, gs_ref, wout_ref,
        lru_conv=lambda cur: conv(cur, lb_ref, cw_ref),
        lru_scan=lambda a, u: _scan_rows(a, u, hl_ref[hdr - 1:hdr, :]),
        sc_conv=lambda cur: conv(cur, sb_ref, sw_ref))
    x1_ref[...] = new_x
    hl_ref[...] = rec[rows - hdr:, :]
    lb_ref[...] = x_lru[rows - hdr:, :]
    sb_ref[...] = cx[rows - hdr:, :]


def _mixer_sample_kernel(x_ref, mod_ref, n1g_ref, win_ref, cw_ref, cb_ref, wa_ref, ba_ref, wx_ref,
                         bx_ref, lam_ref, sw_ref, gl_ref, gs_ref, wout_ref, h0_ref, lb0_ref, sb0_ref,
                         x1_ref, h_ref, lb_ref, sb_ref):
    @pl.when(pl.program_id(0) == 0)
    def _():
        h_ref[...] = h0_ref[...]
        lb_ref[...] = lb0_ref[...]
        sb_ref[...] = sb0_ref[...]

    def conv(cur, hist_ref, w_ref):
        width = w_ref.shape[0]
        out = w_ref[width - 1:width, :] * cur
        for k in range(width - 1):
            out = out + w_ref[k:k + 1, :] * hist_ref[k]
        for k in range(width - 2):
            hist_ref[k] = hist_ref[k + 1]
        hist_ref[width - 2] = cur
        return out

    def scan(a, u):
        h = a * h_ref[...] + u
        h_ref[...] = h
        return h

    new_x, _, _, _ = _mixer_core(
        x_ref[...], mod_ref, n1g_ref, win_ref, cb_ref, wa_ref, ba_ref, wx_ref, bx_ref, lam_ref,
        gl_ref, gs_ref, wout_ref,
        lru_conv=lambda cur: conv(cur, lb_ref, cw_ref),
        lru_scan=scan,
        sc_conv=lambda cur: conv(cur, sb_ref, sw_ref))
    x1_ref[...] = new_x


def _mixer_weight_specs(lw):
    return [_const_spec(lw[k].shape) for k in
            ("norm1_g", "w_in", "lru_conv_w", "lru_conv_b", "wa_bd", "lru_ba", "wx_bd", "lru_bx",
             "lru_lambda", "sconv_w", "gnorm_lru_g", "gnorm_sc_g", "w_out")]


def _mixer_weight_args(lw):
    return [lw[k] for k in
            ("norm1_g", "w_in", "lru_conv_w", "lru_conv_b", "wa_bd", "lru_ba", "wx_bd", "lru_bx",
             "lru_lambda", "sconv_w", "gnorm_lru_g", "gnorm_sc_g", "w_out")]


def _mixer_prompt(x_flat, mod_p, lw, batch, seq):
    t, d = x_flat.shape
    w = lw["lru_conv_b"].shape[-1]
    v = lw["gnorm_sc_g"].shape[-1]
    r = PROMPT_TILE
    tps = seq // r
    state = lambda c: pl.BlockSpec((None, SUBLANES, c), lambda i: (i // tps, 0, 0))
    return pl.pallas_call(
        functools.partial(_mixer_prompt_kernel, tiles_per_seq=tps),
        grid=(t // r,),
        in_specs=[pl.BlockSpec((r, d), lambda i: (i, 0)),
                  pl.BlockSpec((6, None, 1, d), lambda i: (0, i // tps, 0, 0))] + _mixer_weight_specs(lw),
        out_specs=[pl.BlockSpec((r, d), lambda i: (i, 0)), state(w), state(w), state(v)],
        out_shape=[jax.ShapeDtypeStruct((t, d), F32),
                   jax.ShapeDtypeStruct((batch, SUBLANES, w), F32),
                   jax.ShapeDtypeStruct((batch, SUBLANES, w), F32),
                   jax.ShapeDtypeStruct((batch, SUBLANES, v), F32)],
        compiler_params=_params(("arbitrary",), 60),
    )(x_flat, mod_p, *_mixer_weight_args(lw))


def _mixer_sample(x_flat, mod_s, lw, h0, lb0, sb0, batch):
    t, d = x_flat.shape
    w = lw["lru_conv_b"].shape[-1]
    v = lw["gnorm_sc_g"].shape[-1]
    full = lambda shape: pl.BlockSpec(shape, lambda i: (0,) * len(shape))
    return pl.pallas_call(
        _mixer_sample_kernel,
        grid=(t // batch,),
        in_specs=[pl.BlockSpec((batch, d), lambda i: (i, 0)), _const_spec(mod_s.shape)]
        + _mixer_weight_specs(lw) + [_const_spec(h0.shape), _const_spec(lb0.shape), _const_spec(sb0.shape)],
        out_specs=[pl.BlockSpec((batch, d), lambda i: (i, 0)), full(h0.shape), full(lb0.shape), full(sb0.shape)],
        out_shape=[jax.ShapeDtypeStruct((t, d), F32),
                   jax.ShapeDtypeStruct(h0.shape, F32),
                   jax.ShapeDtypeStruct(lb0.shape, F32),
                   jax.ShapeDtypeStruct(sb0.shape, F32)],
        compiler_params=_params(("arbitrary",), 60),
    )(x_flat, mod_s, *_mixer_weight_args(lw), h0, lb0, sb0)


def _oddeven_merge_sort_pairs(n):
    def merge(lo, hi, r):
        step = r * 2
        if step < hi - lo:
            yield from merge(lo, hi, step)
            yield from merge(lo + r, hi, step)
            yield from ((i, i + r) for i in range(lo + r, hi - r, step))
        else:
            yield (lo, lo + r)

    def sort(lo, hi):
        if hi - lo >= 1:
            mid = lo + (hi - lo) // 2
            yield from sort(lo, mid)
            yield from sort(mid + 1, hi)
            yield from merge(lo, hi, 1)

    return list(sort(0, n - 1))


def _top_values(work, k):
    n, r = work.shape
    assert k == 2 * SUBLANES and n <= k * SUBLANES and n % SUBLANES == 0
    w = [work[v * SUBLANES:(v + 1) * SUBLANES, :] for v in range(n // SUBLANES)]
    w += [jnp.full((SUBLANES, r), -jnp.inf, F32)] * (k - len(w))
    for i, j in _oddeven_merge_sort_pairs(k):
        w[i], w[j] = jnp.maximum(w[i], w[j]), jnp.minimum(w[i], w[j])
    shift = SUBLANES // 2
    while shift >= 1:
        w = [jnp.maximum(w[v], pltpu.roll(w[k - 1 - v], shift, axis=0)) for v in range(k)]
        dist = k // 2
        while dist >= 1:
            for v in range(k):
                if v & dist == 0:
                    w[v], w[v + dist] = jnp.maximum(w[v], w[v + dist]), jnp.minimum(w[v], w[v + dist])
            dist //= 2
        shift //= 2
    row = lax.broadcasted_iota(jnp.int32, (SUBLANES, 1), 0)
    halves = []
    for h in range(k // SUBLANES):
        out = w[h * SUBLANES]
        for v in range(1, SUBLANES):
            out = jnp.where(row == v, w[h * SUBLANES + v], out)
        halves.append(out)
    return jnp.concatenate(halves, axis=0)


def _prep_kernel(x1_ref, mod_ref, n2g_ref, wq_ref, sk_ref, h2t_ref, kc_ref, e1_ref, r2_ref, e2_ref):
    nk = sk_ref.shape[1]
    n_heads = wq_ref.shape[1] // (2 * nk)
    k = PEER_TOPK
    shift2, scale2 = mod_ref[3], mod_ref[4]
    h2 = _rms(x1_ref[...], n2g_ref[...]) * (1.0 + scale2) + shift2
    h2t_ref[...] = h2.T.astype(BF16)
    qb = _dot(h2.astype(BF16), wq_ref[...]).astype(BF16)
    nt = (((1,), (1,)), ((), ()))
    for hd in range(n_heads):
        s1, s2 = [lax.dot_general(sk_ref[p], qb[:, (2 * hd + p) * nk:(2 * hd + p + 1) * nk], nt,
                                  preferred_element_type=F32) for p in range(2)]
        v1 = _top_values(s1, k)
        v2 = _top_values(s2, k)
        rank2 = jnp.full(s2.shape, float(nk), F32)
        for b in range(k):
            rank2 = jnp.where(s2 == v2[b:b + 1, :], float(b), rank2)
        blocks = [v1[0:1, :] + v2]
        for a in range(1, SUBLANES):
            blocks.append(v1[a:a + 1, :] + v2[0:SUBLANES, :])
        blocks.append(v1[SUBLANES:, :] + v2[0:1, :])
        top = _top_values(jnp.concatenate(blocks, axis=0), k)
        z = jnp.sum(jnp.exp(top - top[0:1, :]), axis=0, keepdims=True)
        tau = top[k - 1:k, :]
        count_a = jnp.zeros_like(v1)
        for b in range(k):
            count_a = count_a + jnp.where(v1 + v2[b:b + 1, :] >= tau, 1.0, 0.0)
        count = jnp.zeros_like(s1)
        for a in range(k):
            count = jnp.where(s1 == v1[a:a + 1, :], count_a[a:a + 1, :], count)
        kc_ref[hd] = count
        e1_ref[hd] = jnp.exp(s1 - v1[0:1, :]) * (1.0 / z)
        r2_ref[hd * nk:(hd + 1) * nk, :] = rank2.astype(BF16)
        e2_ref[hd * nk:(hd + 1) * nk, :] = jnp.exp(s2 - v2[0:1, :]).astype(BF16)


def _peer_prep(x1_flat, mod, mod_spec, rows, lw):
    t, d = x1_flat.shape
    nq = lw["peer_wq"].shape[1]
    nk = lw["peer_sub_keys"].shape[1]
    n_heads = nq // (2 * nk)
    return pl.pallas_call(
        _prep_kernel,
        grid=(t // rows,),
        in_specs=[pl.BlockSpec((rows, d), lambda i: (i, 0)), mod_spec,
                  _const_spec(lw["norm2_g"].shape), _const_spec(lw["peer_wq"].shape),
                  _const_spec(lw["peer_sub_keys"].shape)],
        out_specs=[pl.BlockSpec((d, rows), lambda i: (0, i)),
                   pl.BlockSpec((n_heads, nk, rows), lambda i: (0, 0, i)),
                   pl.BlockSpec((n_heads, nk, rows), lambda i: (0, 0, i)),
                   pl.BlockSpec((n_heads * nk, rows), lambda i: (0, i)),
                   pl.BlockSpec((n_heads * nk, rows), lambda i: (0, i))],
        out_shape=[jax.ShapeDtypeStruct((d, t), BF16),
                   jax.ShapeDtypeStruct((n_heads, nk, t), F32),
                   jax.ShapeDtypeStruct((n_heads, nk, t), F32),
                   jax.ShapeDtypeStruct((n_heads * nk, t), BF16),
                   jax.ShapeDtypeStruct((n_heads * nk, t), BF16)],
        compiler_params=_params(("arbitrary",), 48),
    )(x1_flat, mod, lw["norm2_g"], lw["peer_wq"], lw["peer_sub_keys"])


def _peer_kernel(h2t_ref, kc_ref, e1_ref, r2_ref, e2_ref, u_ref, vt_ref, x1_ref, g2_ref, fg_ref, y_ref,
                 bc_ref, s0_ref, s1_ref, a_ref, out_ref, *, n_heads, nk, n_chunks, final_norm):
    g = pl.program_id(0)
    c_gate = jnp.maximum(g - 1, 0) % n_chunks
    ec, tb = s0_ref.shape
    i_group = bc_ref.shape[0]
    s_rows = i_group * nk
    jrows = bc_ref.shape[2]
    gl = min(PEER_GATE_LANES, tb)

    @pl.when(g == 0)
    def _():
        s1_ref[...] = jnp.zeros_like(s1_ref)

    @pl.when(c_gate == 0)
    def _():
        out_ref[...] = jnp.zeros_like(out_ref)

    def run(s_w, s_r):
        def gate_piece(r1, jq, l0):
            tiles = [jq * PEER_J_TILES + jt for jt in range(PEER_J_TILES)]
            gates = [[jnp.zeros((jrows, gl), BF16) for _ in tiles] for _ in range(i_group)]
            for hd in range(n_heads):
                kcb = [bc_ref[ii, 2 * hd, :, l0:l0 + gl] for ii in range(i_group)]
                e1b = [bc_ref[ii, 2 * hd + 1, :, l0:l0 + gl] for ii in range(i_group)]
                for n, jt in enumerate(tiles):
                    j0 = hd * nk + jt * jrows
                    r2 = r2_ref[j0:j0 + jrows, l0:l0 + gl]
                    e2 = e2_ref[j0:j0 + jrows, l0:l0 + gl]
                    for ii in range(i_group):
                        gates[ii][n] = gates[ii][n] + jnp.where(r2 < kcb[ii], e2 * e1b[ii], 0.0)
            for ii in range(i_group):
                for n, jt in enumerate(tiles):
                    row = r1 + ii * nk + jt * jrows
                    act = jax.nn.gelu(s_r[row:row + jrows, l0:l0 + gl]) * gates[ii][n]
                    a_ref[row:row + jrows, l0:l0 + gl] = act

        s_w[...] = _dot(u_ref[...], h2t_ref[...]).astype(BF16)
        for it in range(ec // s_rows):
            r1 = it * s_rows
            for ii in range(i_group):
                il = it * i_group + ii
                for hd in range(n_heads):
                    bc_ref[ii, 2 * hd] = jnp.broadcast_to(kc_ref[hd, il:il + 1, :].astype(BF16), (jrows, tb))
                    bc_ref[ii, 2 * hd + 1] = jnp.broadcast_to(e1_ref[hd, il:il + 1, :].astype(BF16), (jrows, tb))
            for jq, l0 in itertools.product(range(nk // (jrows * PEER_J_TILES)), range(0, tb, gl)):
                gate_piece(r1, jq, l0)
            out_ref[...] += _dot(vt_ref[:, r1:r1 + s_rows], a_ref[r1:r1 + s_rows, :])

    @pl.when(g % 2 == 0)
    def _():
        run(s0_ref, s1_ref)

    @pl.when(g % 2 == 1)
    def _():
        run(s1_ref, s0_ref)

    @pl.when((g > 0) & (c_gate == n_chunks - 1))
    def _():
        rows = g2_ref.shape[0]
        r_n = min(max(rows, MXU_DIM), tb)
        for r0 in range(0, tb, r_n):
            peer = out_ref[:, r0:r0 + r_n].T
            gate2 = g2_ref[...] if rows in (1, r_n) else jnp.concatenate([g2_ref[...]] * (r_n // rows), axis=0)
            xo = x1_ref[r0:r0 + r_n, :] + gate2 * peer
            y_ref[r0:r0 + r_n, :] = _rms(xo, fg_ref[...]) if final_norm else xo


def _peer_experts(h2t, kc, e1, r2, e2, u_bf, vt_bf, x1_flat, gate2, gate2_spec, final_g, final_norm):
    d, t = h2t.shape
    n_heads, nk, _ = kc.shape
    ne = u_bf.shape[0]
    tb = min(PEER_TOKENS, t)
    ec = PEER_EXPERTS
    nc = ne // ec
    n = (t // tb) * nc
    gate_step = lambda g: jnp.maximum(g - 1, 0)
    chunk_tab = pl.BlockSpec((n_heads, ec // nk, tb), lambda g: (0, gate_step(g) % nc, gate_step(g) // nc))
    token_tab = pl.BlockSpec((n_heads * nk, tb), lambda g: (0, gate_step(g) // nc),
                             pipeline_mode=pl.Buffered(1))
    return pl.pallas_call(
        functools.partial(_peer_kernel, n_heads=n_heads, nk=nk, n_chunks=nc, final_norm=final_norm),
        grid=(n + 1,),
        in_specs=[pl.BlockSpec((d, tb), lambda g: (0, jnp.minimum(g, n - 1) // nc),
                               pipeline_mode=pl.Buffered(1)),
                  chunk_tab, chunk_tab, token_tab, token_tab,
                  pl.BlockSpec((ec, d), lambda g: (jnp.minimum(g, n - 1) % nc, 0)),
                  pl.BlockSpec((None, d, ec), lambda g: (gate_step(g) % nc, 0, 0)),
                  pl.BlockSpec((tb, d), lambda g: (gate_step(g) // nc, 0), pipeline_mode=pl.Buffered(1)),
                  gate2_spec(lambda g: gate_step(g) // nc), _const_spec(final_g.shape)],
        out_specs=pl.BlockSpec((tb, d), lambda g: (gate_step(g) // nc, 0)),
        out_shape=jax.ShapeDtypeStruct((t, d), F32),
        scratch_shapes=[pltpu.VMEM((PEER_I_GROUP, 2 * n_heads, 2 * SUBLANES, tb), BF16),
                        pltpu.VMEM((ec, tb), BF16),
                        pltpu.VMEM((ec, tb), BF16),
                        pltpu.VMEM((ec, tb), BF16),
                        pltpu.VMEM((d, tb), F32)],
        compiler_params=_params(("arbitrary",), 56),
    )(h2t, kc, e1, r2, e2, u_bf, vt_bf, x1_flat, gate2, final_g)


def _block_diag(w):
    heads, hd, _ = w.shape
    per = MXU_DIM // hd
    wg = w.reshape(heads // per, per, hd, hd)
    eye = jnp.eye(per, dtype=w.dtype)
    return jnp.einsum("gpij,pq->gpiqj", wg, eye).reshape(heads // per, MXU_DIM, MXU_DIM).astype(BF16)


def kernel(x_prompt, x_sample, c_prompt, c_sample, state_lru_h, state_lru_conv, state_sconv, w_ada, b_ada, norm1_g, norm2_g, w_in, lru_conv_w, lru_conv_b, lru_wa, lru_ba, lru_wx, lru_bx, lru_lambda, sconv_w, gnorm_lru_g, gnorm_sc_g, w_out, peer_wq, peer_sub_keys, peer_u, peer_v, final_g):
    bp, seq, d = x_prompt.shape
    bs, sseq, _ = x_sample.shape
    depth = w_ada.shape[0]
    nk = peer_sub_keys.shape[2]
    n_heads = peer_wq.shape[2] // (2 * nk)
    tps = seq // PROMPT_TILE

    xp = x_prompt.reshape(bp * seq, d)
    xs = jnp.swapaxes(x_sample, 0, 1).reshape(sseq * bs, d)
    n_c = bp + bs
    pad = (-n_c) % SUBLANES
    c_all = jnp.concatenate([c_prompt, c_sample, jnp.zeros((pad, d), F32)], axis=0)
    fg = final_g.reshape(1, d)

    mod_p_spec = pl.BlockSpec((6, None, 1, d), lambda i: (0, i // tps, 0, 0))
    mod_s_spec = _const_spec((6, bs, d))
    blocks_per_seq = seq // PEER_TOKENS
    gate2_p_spec = lambda blk: pl.BlockSpec((None, None, 1, d), lambda g: (5, blk(g) // blocks_per_seq, 0, 0))
    gate2_s_spec = lambda blk: pl.BlockSpec((None, bs, d), lambda g: (5, 0, 0), pipeline_mode=pl.Buffered(1))

    outs = [[] for _ in range(6)]
    for l in range(depth):
        row = lambda a: a[l].reshape(1, -1)
        lw = dict(
            norm1_g=row(norm1_g), norm2_g=row(norm2_g), w_in=w_in[l].astype(BF16),
            lru_conv_w=lru_conv_w[l], lru_conv_b=row(lru_conv_b),
            wa_bd=_block_diag(lru_wa[l]), lru_ba=row(lru_ba),
            wx_bd=_block_diag(lru_wx[l]), lru_bx=row(lru_bx), lru_lambda=row(lru_lambda),
            sconv_w=sconv_w[l], gnorm_lru_g=row(gnorm_lru_g), gnorm_sc_g=row(gnorm_sc_g),
            w_out=w_out[l].astype(BF16), peer_wq=peer_wq[l].astype(BF16),
            peer_sub_keys=peer_sub_keys[l].astype(BF16))
        u_bf = peer_u[l].astype(BF16)
        ne = peer_v.shape[1]
        vt_bf = jnp.swapaxes(peer_v[l].reshape(ne // PEER_EXPERTS, PEER_EXPERTS, d), 1, 2).astype(BF16)

        mod = _adaln(c_all, w_ada[l], b_ada[l])
        mod_p = jnp.swapaxes(mod[:bp].reshape(bp, 6, d), 0, 1).reshape(6, bp, 1, d)
        mod_s = jnp.swapaxes(mod[bp:n_c].reshape(bs, 6, d), 0, 1)

        xp1, hl, lb, sb = _mixer_prompt(xp, mod_p, lw, bp, seq)
        h0 = state_lru_h[l]
        lb0 = jnp.swapaxes(state_lru_conv[l], 0, 1)
        sb0 = jnp.swapaxes(state_sconv[l], 0, 1)
        xs1, hs, lbs, sbs = _mixer_sample(xs, mod_s, lw, h0, lb0, sb0, bs)

        last = l == depth - 1
        new_x = []
        for x1, mod_g, spec, rows, g2_spec in ((xp1, mod_p, mod_p_spec, PROMPT_TILE, gate2_p_spec),
                                               (xs1, mod_s, mod_s_spec, bs, gate2_s_spec)):
            new_x.append(_peer_experts(*_peer_prep(x1, mod_g, spec, rows, lw), u_bf, vt_bf,
                                       x1, mod_g, g2_spec, fg, last))
        xp, xs = new_x

        nl = lru_conv_w.shape[1] - 1
        ns = sconv_w.shape[1] - 1
        outs[0].append(hl[:, SUBLANES - 1, :])
        outs[1].append(lb[:, SUBLANES - nl:, :])
        outs[2].append(sb[:, SUBLANES - ns:, :])
        outs[3].append(hs)
        outs[4].append(jnp.swapaxes(lbs, 0, 1))
        outs[5].append(jnp.swapaxes(sbs, 0, 1))

    y_prompt = xp.reshape(bp, seq, d)
    y_sample = jnp.swapaxes(xs.reshape(sseq, bs, d), 0, 1)
    return (y_prompt, y_sample) + tuple(jnp.stack(o) for o in outs)
```

```python
import functools
import itertools

import jax
import jax.numpy as jnp
from jax import lax
from jax.experimental import pallas as pl
from jax.experimental.pallas import tpu as pltpu

F32 = jnp.float32
BF16 = jnp.bfloat16

EPS = 1e-6
LRU_C = 8.0
PEER_TOPK = 16

SUBLANES = 8
MXU_DIM = 256

PROMPT_TILE = 256
PEER_TOKENS = 512
PEER_EXPERTS = 1024
PEER_I_GROUP = 2
PEER_GATE_LANES = 256
PEER_J_TILES = 2
ADALN_COLS = 1024


def _dot(a, b):
    return jnp.dot(a, b, preferred_element_type=F32)


def _rms(x, g):
    ms = jnp.mean(x * x, axis=-1, keepdims=True)
    return x * lax.rsqrt(ms + EPS) * g


def _const_spec(shape):
    nd = len(shape)
    return pl.BlockSpec(shape, lambda *_: (0,) * nd, pipeline_mode=pl.Buffered(1))


def _params(sem, vmem_mb):
    return pltpu.CompilerParams(dimension_semantics=sem, vmem_limit_bytes=vmem_mb * 1024 * 1024)


def _adaln_kernel(c_ref, w_ref, b_ref, o_ref):
    sc = jax.nn.silu(c_ref[...]).astype(BF16)
    o_ref[...] = _dot(sc, w_ref[...].astype(BF16)) + b_ref[...]


def _adaln(c_all, w_ada, b_ada):
    m, d = c_all.shape
    n = w_ada.shape[1]
    tn = ADALN_COLS
    return pl.pallas_call(
        _adaln_kernel,
        grid=(n // tn,),
        in_specs=[pl.BlockSpec((m, d), lambda j: (0, 0)),
                  pl.BlockSpec((d, tn), lambda j: (0, j)),
                  pl.BlockSpec((1, tn), lambda j: (0, j))],
        out_specs=pl.BlockSpec((m, tn), lambda j: (0, j)),
        out_shape=jax.ShapeDtypeStruct((m, n), F32),
        compiler_params=_params(("arbitrary",), 40),
    )(c_all, w_ada, b_ada.reshape(1, n))


def _gate_pre(xcb, w_ref, b_ref):
    nblk = w_ref.shape[0]
    parts = [_dot(xcb[:, g * MXU_DIM:(g + 1) * MXU_DIM], w_ref[g]) for g in range(nblk)]
    return jnp.concatenate(parts, axis=1) + b_ref[...]


def _scan_rows(a, u, h_in):
    rows = a.shape[0]
    row = lax.broadcasted_iota(jnp.int32, (rows, 1), 0)
    acc_a = a
    acc_b = u + jnp.where(row == 0, a * h_in, 0.0)
    d = 1
    while d < rows:
        keep = row >= d
        sh_b = pltpu.roll(acc_b, d, axis=0)
        acc_b = acc_b + jnp.where(keep, acc_a * sh_b, 0.0)
        if 2 * d < rows:
            sh_a = pltpu.roll(acc_a, d, axis=0)
            acc_a = jnp.where(keep, acc_a * sh_a, acc_a)
        d *= 2
    return acc_b


def _mixer_core(x, mod_ref, n1g_ref, win_ref, cb_ref, wa_ref, ba_ref, wx_ref, bx_ref, lam_ref,
                gl_ref, gs_ref, wout_ref, lru_conv, lru_scan, sc_conv):
    w = cb_ref.shape[-1]
    v = gs_ref.shape[-1]
    shift1, scale1, gate1 = mod_ref[0], mod_ref[1], mod_ref[2]
    h = _rms(x, n1g_ref[...]) * (1.0 + scale1) + shift1
    hb = h.astype(BF16)

    x_lru = _dot(hb, win_ref[:, 0:w])
    xc = lru_conv(x_lru) + cb_ref[...]
    xcb = xc.astype(BF16)
    r = jax.nn.sigmoid(_gate_pre(xcb, wa_ref, ba_ref))
    i = jax.nn.sigmoid(_gate_pre(xcb, wx_ref, bx_ref))
    log_a = (-LRU_C) * r * jax.nn.softplus(-lam_ref[...])
    a = jnp.exp(log_a)
    u = jnp.sqrt(jnp.tanh(-log_a) * (1.0 + a * a)) * (i * xc)
    rec = lru_scan(a, u)
    y_gate = _dot(hb, win_ref[:, w:2 * w])
    out_lru = rec * jax.nn.gelu(y_gate)

    sc_b = _dot(hb, win_ref[:, 2 * w:2 * w + v])
    sc_c = _dot(hb, win_ref[:, 2 * w + v:2 * w + 2 * v])
    sc_x = _dot(hb, win_ref[:, 2 * w + 2 * v:2 * w + 3 * v])
    cx = sc_c * sc_x
    out_sc = sc_b * sc_conv(cx)

    n_l = _rms(out_lru, gl_ref[...]).astype(BF16)
    n_s = _rms(out_sc, gs_ref[...]).astype(BF16)
    mix = _dot(n_l, wout_ref[0:w, :]) + _dot(n_s, wout_ref[w:w + v, :])
    return x + gate1 * mix, x_lru, rec, cx


def _mixer_prompt_kernel(x_ref, mod_ref, n1g_ref, win_ref, cw_ref, cb_ref, wa_ref, ba_ref, wx_ref,
                         bx_ref, lam_ref, sw_ref, gl_ref, gs_ref, wout_ref,
                         x1_ref, hl_ref, lb_ref, sb_ref, *, tiles_per_seq):
    rows = x_ref.shape[0]
    hdr = SUBLANES

    @pl.when(pl.program_id(0) % tiles_per_seq == 0)
    def _():
        hl_ref[...] = jnp.zeros_like(hl_ref)
        lb_ref[...] = jnp.zeros_like(lb_ref)
        sb_ref[...] = jnp.zeros_like(sb_ref)

    def conv(cur, tail_ref, w_ref):
        width = w_ref.shape[0]
        ext = jnp.concatenate([tail_ref[...], cur], axis=0)
        out = w_ref[width - 1:width, :] * cur
        for d in range(1, width):
            out = out + w_ref[width - 1 - d:width - d, :] * pltpu.roll(ext, d, axis=0)[hdr:]
        return out

    new_x, x_lru, rec, cx = _mixer_core(
        x_ref[...], mod_ref, n1g_ref, win_ref, cb_ref, wa_ref, ba_ref, wx_ref, bx_ref, lam_ref,
        gl_ref, gs_ref, wout_ref,
        lru_conv=lambda cur: conv(cur, lb_ref, cw_ref),
        lru_scan=lambda a, u: _scan_rows(a, u, hl_ref[hdr - 1:hdr, :]),
        sc_conv=lambda cur: conv(cur, sb_ref, sw_ref))
    x1_ref[...] = new_x
    hl_ref[...] = rec[rows - hdr:, :]
    lb_ref[...] = x_lru[rows - hdr:, :]
    sb_ref[...] = cx[rows - hdr:, :]


def _mixer_sample_kernel(x_ref, mod_ref, n1g_ref, win_ref, cw_ref, cb_ref, wa_ref, ba_ref, wx_ref,
                         bx_ref, lam_ref, sw_ref, gl_ref, gs_ref, wout_ref, h0_ref, lb0_ref, sb0_ref,
                         x1_ref, h_ref, lb_ref, sb_ref):
    @pl.when(pl.program_id(0) == 0)
    def _():
        h_ref[...] = h0_ref[...]
        lb_ref[...] = lb0_ref[...]
        sb_ref[...] = sb0_ref[...]

    def conv(cur, hist_ref, w_ref):
        width = w_ref.shape[0]
        out = w_ref[width - 1:width, :] * cur
        for k in range(width - 1):
            out = out + w_ref[k:k + 1, :] * hist_ref[k]
        for k in range(width - 2):
            hist_ref[k] = hist_ref[k + 1]
        hist_ref[width - 2] = cur
        return out

    def scan(a, u):
        h = a * h_ref[...] + u
        h_ref[...] = h
        return h

    new_x, _, _, _ = _mixer_core(
        x_ref[...], mod_ref, n1g_ref, win_ref, cb_ref, wa_ref, ba_ref, wx_ref, bx_ref, lam_ref,
        gl_ref, gs_ref, wout_ref,
        lru_conv=lambda cur: conv(cur, lb_ref, cw_ref),
        lru_scan=scan,
        sc_conv=lambda cur: conv(cur, sb_ref, sw_ref))
    x1_ref[...] = new_x


def _mixer_weight_specs(lw):
    return [_const_spec(lw[k].shape) for k in
            ("norm1_g", "w_in", "lru_conv_w", "lru_conv_b", "wa_bd", "lru_ba", "wx_bd", "lru_bx",
             "lru_lambda", "sconv_w", "gnorm_lru_g", "gnorm_sc_g", "w_out")]


def _mixer_weight_args(lw):
    return [lw[k] for k in
            ("norm1_g", "w_in", "lru_conv_w", "lru_conv_b", "wa_bd", "lru_ba", "wx_bd", "lru_bx",
             "lru_lambda", "sconv_w", "gnorm_lru_g", "gnorm_sc_g", "w_out")]


def _mixer_prompt(x_flat, mod_p, lw, batch, seq):
    t, d = x_flat.shape
    w = lw["lru_conv_b"].shape[-1]
    v = lw["gnorm_sc_g"].shape[-1]
    r = PROMPT_TILE
    tps = seq // r
    state = lambda c: pl.BlockSpec((None, SUBLANES, c), lambda i: (i // tps, 0, 0))
    return pl.pallas_call(
        functools.partial(_mixer_prompt_kernel, tiles_per_seq=tps),
        grid=(t // r,),
        in_specs=[pl.BlockSpec((r, d), lambda i: (i, 0)),
                  pl.BlockSpec((6, None, 1, d), lambda i: (0, i // tps, 0, 0))] + _mixer_weight_specs(lw),
        out_specs=[pl.BlockSpec((r, d), lambda i: (i, 0)), state(w), state(w), state(v)],
        out_shape=[jax.ShapeDtypeStruct((t, d), F32),
                   jax.ShapeDtypeStruct((batch, SUBLANES, w), F32),
                   jax.ShapeDtypeStruct((batch, SUBLANES, w), F32),
                   jax.ShapeDtypeStruct((batch, SUBLANES, v), F32)],
        compiler_params=_params(("arbitrary",), 60),
    )(x_flat, mod_p, *_mixer_weight_args(lw))


def _mixer_sample(x_flat, mod_s, lw, h0, lb0, sb0, batch):
    t, d = x_flat.shape
    w = lw["lru_conv_b"].shape[-1]
    v = lw["gnorm_sc_g"].shape[-1]
    full = lambda shape: pl.BlockSpec(shape, lambda i: (0,) * len(shape))
    return pl.pallas_call(
        _mixer_sample_kernel,
        grid=(t // batch,),
        in_specs=[pl.BlockSpec((batch, d), lambda i: (i, 0)), _const_spec(mod_s.shape)]
        + _mixer_weight_specs(lw) + [_const_spec(h0.shape), _const_spec(lb0.shape), _const_spec(sb0.shape)],
        out_specs=[pl.BlockSpec((batch, d), lambda i: (i, 0)), full(h0.shape), full(lb0.shape), full(sb0.shape)],
        out_shape=[jax.ShapeDtypeStruct((t, d), F32),
                   jax.ShapeDtypeStruct(h0.shape, F32),
                   jax.ShapeDtypeStruct(lb0.shape, F32),
                   jax.ShapeDtypeStruct(sb0.shape, F32)],
        compiler_params=_params(("arbitrary",), 60),
    )(x_flat, mod_s, *_mixer_weight_args(lw), h0, lb0, sb0)


def _oddeven_merge_sort_pairs(n):
    def merge(lo, hi, r):
        step = r * 2
        if step < hi - lo:
            yield from merge(lo, hi, step)
            yield from merge(lo + r, hi, step)
            yield from ((i, i + r) for i in range(lo + r, hi - r, step))
        else:
            yield (lo, lo + r)

    def sort(lo, hi):
        if hi - lo >= 1:
            mid = lo + (hi - lo) // 2
            yield from sort(lo, mid)
            yield from sort(mid + 1, hi)
            yield from merge(lo, hi, 1)

    return list(sort(0, n - 1))


def _top_values(work, k):
    n, r = work.shape
    assert k == 2 * SUBLANES and n <= k * SUBLANES and n % SUBLANES == 0
    w = [work[v * SUBLANES:(v + 1) * SUBLANES, :] for v in range(n // SUBLANES)]
    w += [jnp.full((SUBLANES, r), -jnp.inf, F32)] * (k - len(w))
    for i, j in _oddeven_merge_sort_pairs(k):
        w[i], w[j] = jnp.maximum(w[i], w[j]), jnp.minimum(w[i], w[j])
    shift = SUBLANES // 2
    while shift >= 1:
        w = [jnp.maximum(w[v], pltpu.roll(w[k - 1 - v], shift, axis=0)) for v in range(k)]
        dist = k // 2
        while dist >= 1:
            for v in range(k):
                if v & dist == 0:
                    w[v], w[v + dist] = jnp.maximum(w[v], w[v + dist]), jnp.minimum(w[v], w[v + dist])
            dist //= 2
        shift //= 2
    row = lax.broadcasted_iota(jnp.int32, (SUBLANES, 1), 0)
    halves = []
    for h in range(k // SUBLANES):
        out = w[h * SUBLANES]
        for v in range(1, SUBLANES):
            out = jnp.where(row == v, w[h * SUBLANES + v], out)
        halves.append(out)
    return jnp.concatenate(halves, axis=0)


def _prep_kernel(x1_ref, mod_ref, n2g_ref, wq_ref, sk_ref, h2t_ref, kc_ref, e1_ref, r2_ref, e2_ref):
    nk = sk_ref.shape[1]
    n_heads = wq_ref.shape[1] // (2 * nk)
    k = PEER_TOPK
    shift2, scale2 = mod_ref[3], mod_ref[4]
    h2 = _rms(x1_ref[...], n2g_ref[...]) * (1.0 + scale2) + shift2
    h2t_ref[...] = h2.T.astype(BF16)
    qb = _dot(h2.astype(BF16), wq_ref[...]).astype(BF16)
    nt = (((1,), (1,)), ((), ()))
    for hd in range(n_heads):
        s1, s2 = [lax.dot_general(sk_ref[p], qb[:, (2 * hd + p) * nk:(2 * hd + p + 1) * nk], nt,
                                  preferred_element_type=F32) for p in range(2)]
        v1 = _top_values(s1, k)
        v2 = _top_values(s2, k)
        rank2 = jnp.full(s2.shape, float(nk), F32)
        for b in range(k):
            rank2 = jnp.where(s2 == v2[b:b + 1, :], float(b), rank2)
        blocks = [v1[0:1, :] + v2]
        for a in range(1, SUBLANES):
            blocks.append(v1[a:a + 1, :] + v2[0:SUBLANES, :])
        blocks.append(v1[SUBLANES:, :] + v2[0:1, :])
        top = _top_values(jnp.concatenate(blocks, axis=0), k)
        z = jnp.sum(jnp.exp(top - top[0:1, :]), axis=0, keepdims=True)
        tau = top[k - 1:k, :]
        count_a = jnp.zeros_like(v1)
        for b in range(k):
            count_a = count_a + jnp.where(v1 + v2[b:b + 1, :] >= tau, 1.0, 0.0)
        count = jnp.zeros_like(s1)
        for a in range(k):
            count = jnp.where(s1 == v1[a:a + 1, :], count_a[a:a + 1, :], count)
        kc_ref[hd] = count
        e1_ref[hd] = jnp.exp(s1 - v1[0:1, :]) * (1.0 / z)
        r2_ref[hd * nk:(hd + 1) * nk, :] = rank2.astype(BF16)
        e2_ref[hd * nk:(hd + 1) * nk, :] = jnp.exp(s2 - v2[0:1, :]).astype(BF16)


def _peer_prep(x1_flat, mod, mod_spec, rows, lw):
    t, d = x1_flat.shape
    nq = lw["peer_wq"].shape[1]
    nk = lw["peer_sub_keys"].shape[1]
    n_heads = nq // (2 * nk)
    return pl.pallas_call(
        _prep_kernel,
        grid=(t // rows,),
        in_specs=[pl.BlockSpec((rows, d), lambda i: (i, 0)), mod_spec,
                  _const_spec(lw["norm2_g"].shape), _const_spec(lw["peer_wq"].shape),
                  _const_spec(lw["peer_sub_keys"].shape)],
        out_specs=[pl.BlockSpec((d, rows), lambda i: (0, i)),
                   pl.BlockSpec((n_heads, nk, rows), lambda i: (0, 0, i)),
                   pl.BlockSpec((n_heads, nk, rows), lambda i: (0, 0, i)),
                   pl.BlockSpec((n_heads * nk, rows), lambda i: (0, i)),
                   pl.BlockSpec((n_heads * nk, rows), lambda i: (0, i))],
        out_shape=[jax.ShapeDtypeStruct((d, t), BF16),
                   jax.ShapeDtypeStruct((n_heads, nk, t), F32),
                   jax.ShapeDtypeStruct((n_heads, nk, t), F32),
                   jax.ShapeDtypeStruct((n_heads * nk, t), BF16),
                   jax.ShapeDtypeStruct((n_heads * nk, t), BF16)],
        compiler_params=_params(("arbitrary",), 48),
    )(x1_flat, mod, lw["norm2_g"], lw["peer_wq"], lw["peer_sub_keys"])


def _peer_kernel(h2t_ref, kc_ref, e1_ref, r2_ref, e2_ref, u_ref, vt_ref, x1_ref, g2_ref, fg_ref, y_ref,
                 bc_ref, s0_ref, s1_ref, a_ref, out_ref, *, n_heads, nk, n_chunks, final_norm):
    g = pl.program_id(0)
    c_gate = jnp.maximum(g - 1, 0) % n_chunks
    ec, tb = s0_ref.shape
    i_group = bc_ref.shape[0]
    s_rows = i_group * nk
    jrows = bc_ref.shape[2]
    gl = min(PEER_GATE_LANES, tb)

    @pl.when(g == 0)
    def _():
        s1_ref[...] = jnp.zeros_like(s1_ref)

    @pl.when(c_gate == 0)
    def _():
        out_ref[...] = jnp.zeros_like(out_ref)

    def run(s_w, s_r):
        def gate_piece(r1, jq, l0):
            tiles = [jq * PEER_J_TILES + jt for jt in range(PEER_J_TILES)]
            gates = [[jnp.zeros((jrows, gl), BF16) for _ in tiles] for _ in range(i_group)]
            for hd in range(n_heads):
                kcb = [bc_ref[ii, 2 * hd, :, l0:l0 + gl] for ii in range(i_group)]
                e1b = [bc_ref[ii, 2 * hd + 1, :, l0:l0 + gl] for ii in range(i_group)]
                for n, jt in enumerate(tiles):
                    j0 = hd * nk + jt * jrows
                    r2 = r2_ref[j0:j0 + jrows, l0:l0 + gl]
                    e2 = e2_ref[j0:j0 + jrows, l0:l0 + gl]
                    for ii in range(i_group):
                        gates[ii][n] = gates[ii][n] + jnp.where(r2 < kcb[ii], e2 * e1b[ii], 0.0)
            for ii in range(i_group):
                for n, jt in enumerate(tiles):
                    row = r1 + ii * nk + jt * jrows
                    act = jax.nn.gelu(s_r[row:row + jrows, l0:l0 + gl]) * gates[ii][n]
                    a_ref[row:row + jrows, l0:l0 + gl] = act

        s_w[...] = _dot(u_ref[...], h2t_ref[...]).astype(BF16)
        for it in range(ec // s_rows):
            r1 = it * s_rows
            for ii in range(i_group):
                il = it * i_group + ii
                for hd in range(n_heads):
                    bc_ref[ii, 2 * hd] = jnp.broadcast_to(kc_ref[hd, il:il + 1, :].astype(BF16), (jrows, tb))
                    bc_ref[ii, 2 * hd + 1] = jnp.broadcast_to(e1_ref[hd, il:il + 1, :].astype(BF16), (jrows, tb))
            for jq, l0 in itertools.product(range(nk // (jrows * PEER_J_TILES)), range(0, tb, gl)):
                gate_piece(r1, jq, l0)
            out_ref[...] += _dot(vt_ref[:, r1:r1 + s_rows], a_ref[r1:r1 + s_rows, :])

    @pl.when(g % 2 == 0)
    def _():
        run(s0_ref, s1_ref)

    @pl.when(g % 2 == 1)
    def _():
        run(s1_ref, s0_ref)

    @pl.when((g > 0) & (c_gate == n_chunks - 1))
    def _():
        rows = g2_ref.shape[0]
        r_n = min(max(rows, MXU_DIM), tb)
        for r0 in range(0, tb, r_n):
            peer = out_ref[:, r0:r0 + r_n].T
            gate2 = g2_ref[...] if rows in (1, r_n) else jnp.concatenate([g2_ref[...]] * (r_n // rows), axis=0)
            xo = x1_ref[r0:r0 + r_n, :] + gate2 * peer
            y_ref[r0:r0 + r_n, :] = _rms(xo, fg_ref[...]) if final_norm else xo


def _peer_experts(h2t, kc, e1, r2, e2, u_bf, vt_bf, x1_flat, gate2, gate2_spec, final_g, final_norm):
    d, t = h2t.shape
    n_heads, nk, _ = kc.shape
    ne = u_bf.shape[0]
    tb = min(PEER_TOKENS, t)
    ec = PEER_EXPERTS
    nc = ne // ec
    n = (t // tb) * nc
    gate_step = lambda g: jnp.maximum(g - 1, 0)
    chunk_tab = pl.BlockSpec((n_heads, ec // nk, tb), lambda g: (0, gate_step(g) % nc, gate_step(g) // nc))
    token_tab = pl.BlockSpec((n_heads * nk, tb), lambda g: (0, gate_step(g) // nc),
                             pipeline_mode=pl.Buffered(1))
    return pl.pallas_call(
        functools.partial(_peer_kernel, n_heads=n_heads, nk=nk, n_chunks=nc, final_norm=final_norm),
        grid=(n + 1,),
        in_specs=[pl.BlockSpec((d, tb), lambda g: (0, jnp.minimum(g, n - 1) // nc),
                               pipeline_mode=pl.Buffered(1)),
                  chunk_tab, chunk_tab, token_tab, token_tab,
                  pl.BlockSpec((ec, d), lambda g: (jnp.minimum(g, n - 1) % nc, 0)),
                  pl.BlockSpec((None, d, ec), lambda g: (gate_step(g) % nc, 0, 0)),
                  pl.BlockSpec((tb, d), lambda g: (gate_step(g) // nc, 0), pipeline_mode=pl.Buffered(1)),
                  gate2_spec(lambda g: gate_step(g) // nc), _const_spec(final_g.shape)],
        out_specs=pl.BlockSpec((tb, d), lambda g: (gate_step(g) // nc, 0)),
        out_shape=jax.ShapeDtypeStruct((t, d), F32),
        scratch_shapes=[pltpu.VMEM((PEER_I_GROUP, 2 * n_heads, 2 * SUBLANES, tb), BF16),
                        pltpu.VMEM((ec, tb), BF16),
                        pltpu.VMEM((ec, tb), BF16),
                        pltpu.VMEM((ec, tb), BF16),
                        pltpu.VMEM((d, tb), F32)],
        compiler_params=_params(("arbitrary",), 56),
    )(h2t, kc, e1, r2, e2, u_bf, vt_bf, x1_flat, gate2, final_g)


def _block_diag(w):
    heads, hd, _ = w.shape
    per = MXU_DIM // hd
    wg = w.reshape(heads // per, per, hd, hd)
    eye = jnp.eye(per, dtype=w.dtype)
    return jnp.einsum("gpij,pq->gpiqj", wg, eye).reshape(heads // per, MXU_DIM, MXU_DIM).astype(BF16)


def kernel(x_prompt, x_sample, c_prompt, c_sample, state_lru_h, state_lru_conv, state_sconv, w_ada, b_ada, norm1_g, norm2_g, w_in, lru_conv_w, lru_conv_b, lru_wa, lru_ba, lru_wx, lru_bx, lru_lambda, sconv_w, gnorm_lru_g, gnorm_sc_g, w_out, peer_wq, peer_sub_keys, peer_u, peer_v, final_g):
    bp, seq, d = x_prompt.shape
    bs, sseq, _ = x_sample.shape
    depth = w_ada.shape[0]

    xp = x_prompt.reshape(bp * seq, d)
    xs = jnp.swapaxes(x_sample, 0, 1).reshape(sseq * bs, d)
    n_c = bp + bs
    pad = (-n_c) % SUBLANES
    c_all = jnp.concatenate([c_prompt, c_sample, jnp.zeros((pad, d), F32)], axis=0)
    fg = final_g.reshape(1, d)

    mod_p_spec = pl.BlockSpec((6, None, 1, d), lambda i: (0, i // (seq // PROMPT_TILE), 0, 0))
    mod_s_spec = _const_spec((6, bs, d))
    blocks_per_seq = seq // PEER_TOKENS
    gate2_p_spec = lambda blk: pl.BlockSpec((None, None, 1, d), lambda g: (5, blk(g) // blocks_per_seq, 0, 0))
    gate2_s_spec = lambda blk: pl.BlockSpec((None, bs, d), lambda g: (5, 0, 0), pipeline_mode=pl.Buffered(1))

    outs = [[] for _ in range(6)]
    for l in range(depth):
        row = lambda a: a[l].reshape(1, -1)
        lw = dict(
            norm1_g=row(norm1_g), norm2_g=row(norm2_g), w_in=w_in[l].astype(BF16),
            lru_conv_w=lru_conv_w[l], lru_conv_b=row(lru_conv_b),
            wa_bd=_block_diag(lru_wa[l]), lru_ba=row(lru_ba),
            wx_bd=_block_diag(lru_wx[l]), lru_bx=row(lru_bx), lru_lambda=row(lru_lambda),
            sconv_w=sconv_w[l], gnorm_lru_g=row(gnorm_lru_g), gnorm_sc_g=row(gnorm_sc_g),
            w_out=w_out[l].astype(BF16), peer_wq=peer_wq[l].astype(BF16),
            peer_sub_keys=peer_sub_keys[l].astype(BF16))
        u_bf = peer_u[l].astype(BF16)
        ne = peer_v.shape[1]
        vt_bf = jnp.swapaxes(peer_v[l].reshape(ne // PEER_EXPERTS, PEER_EXPERTS, d), 1, 2).astype(BF16)

        mod = _adaln(c_all, w_ada[l], b_ada[l])
        mod_p = jnp.swapaxes(mod[:bp].reshape(bp, 6, d), 0, 1).reshape(6, bp, 1, d)
        mod_s = jnp.swapaxes(mod[bp:n_c].reshape(bs, 6, d), 0, 1)

        xp1, hl, lb, sb = _mixer_prompt(xp, mod_p, lw, bp, seq)
        h0 = state_lru_h[l]
        lb0 = jnp.swapaxes(state_lru_conv[l], 0, 1)
        sb0 = jnp.swapaxes(state_sconv[l], 0, 1)
        xs1, hs, lbs, sbs = _mixer_sample(xs, mod_s, lw, h0, lb0, sb0, bs)

        last = l == depth - 1
        new_x = []
        for x1, mod_g, spec, rows, g2_spec in ((xp1, mod_p, mod_p_spec, PROMPT_TILE, gate2_p_spec),
                                               (xs1, mod_s, mod_s_spec, bs, gate2_s_spec)):
            new_x.append(_peer_experts(*_peer_prep(x1, mod_g, spec, rows, lw), u_bf, vt_bf,
                                       x1, mod_g, g2_spec, fg, last))
        xp, xs = new_x

        nl = lru_conv_w.shape[1] - 1
        ns = sconv_w.shape[1] - 1
        outs[0].append(hl[:, SUBLANES - 1, :])
        outs[1].append(lb[:, SUBLANES - nl:, :])
        outs[2].append(sb[:, SUBLANES - ns:, :])
        outs[3].append(hs)
        outs[4].append(jnp.swapaxes(lbs, 0, 1))
        outs[5].append(jnp.swapaxes(sbs, 0, 1))

    y_prompt = xp.reshape(bp, seq, d)
    y_sample = jnp.swapaxes(xs.reshape(sseq, bs, d), 0, 1)
    return (y_prompt, y_sample) + tuple(jnp.stack(o) for o in outs)
```

```python
import functools
import itertools

import jax
import jax.numpy as jnp
from jax import lax
from jax.experimental import pallas as pl
from jax.experimental.pallas import tpu as pltpu

F32 = jnp.float32
BF16 = jnp.bfloat16

EPS = 1e-6
LRU_C = 8.0
PEER_TOPK = 16

SUBLANES = 8
MXU_DIM = 256

PROMPT_TILE = 256
PEER_TOKENS = 512
PEER_EXPERTS = 1024
PEER_I_GROUP = 2
PEER_GATE_LANES = 256
PEER_J_TILES = 4
ADALN_COLS = 1024


def _dot(a, b):
    return jnp.dot(a, b, preferred_element_type=F32)


def _rms(x, g):
    ms = jnp.mean(x * x, axis=-1, keepdims=True)
    return x * lax.rsqrt(ms + EPS) * g


def _const_spec(shape):
    nd = len(shape)
    return pl.BlockSpec(shape, lambda *_: (0,) * nd, pipeline_mode=pl.Buffered(1))


def _params(sem, vmem_mb):
    return pltpu.CompilerParams(dimension_semantics=sem, vmem_limit_bytes=vmem_mb * 1024 * 1024)


def _adaln_kernel(c_ref, w_ref, b_ref, o_ref):
    sc = jax.nn.silu(c_ref[...]).astype(BF16)
    o_ref[...] = _dot(sc, w_ref[...].astype(BF16)) + b_ref[...]


def _adaln(c_all, w_ada, b_ada):
    m, d = c_all.shape
    n = w_ada.shape[1]
    tn = ADALN_COLS
    return pl.pallas_call(
        _adaln_kernel,
        grid=(n // tn,),
        in_specs=[pl.BlockSpec((m, d), lambda j: (0, 0)),
                  pl.BlockSpec((d, tn), lambda j: (0, j)),
                  pl.BlockSpec((1, tn), lambda j: (0, j))],
        out_specs=pl.BlockSpec((m, tn), lambda j: (0, j)),
        out_shape=jax.ShapeDtypeStruct((m, n), F32),
        compiler_params=_params(("arbitrary",), 40),
    )(c_all, w_ada, b_ada.reshape(1, n))


def _gate_pre(xcb, w_ref, b_ref):
    nblk = w_ref.shape[0]
    parts = [_dot(xcb[:, g * MXU_DIM:(g + 1) * MXU_DIM], w_ref[g]) for g in range(nblk)]
    return jnp.concatenate(parts, axis=1) + b_ref[...]


def _scan_rows(a, u, h_in):
    rows = a.shape[0]
    row = lax.broadcasted_iota(jnp.int32, (rows, 1), 0)
    acc_a = a
    acc_b = u + jnp.where(row == 0, a * h_in, 0.0)
    d = 1
    while d < rows:
        keep = row >= d
        sh_b = pltpu.roll(acc_b, d, axis=0)
        acc_b = acc_b + jnp.where(keep, acc_a * sh_b, 0.0)
        if 2 * d < rows:
            sh_a = pltpu.roll(acc_a, d, axis=0)
            acc_a = jnp.where(keep, acc_a * sh_a, acc_a)
        d *= 2
    return acc_b


def _mixer_core(x, mod_ref, n1g_ref, win_ref, cb_ref, wa_ref, ba_ref, wx_ref, bx_ref, lam_ref,
                gl_ref, gs_ref, wout_ref, lru_conv, lru_scan, sc_conv):
    w = cb_ref.shape[-1]
    v = gs_ref.shape[-1]
    shift1, scale1, gate1 = mod_ref[0], mod_ref[1], mod_ref[2]
    h = _rms(x, n1g_ref[...]) * (1.0 + scale1) + shift1
    hb = h.astype(BF16)

    x_lru = _dot(hb, win_ref[:, 0:w])
    xc = lru_conv(x_lru) + cb_ref[...]
    xcb = xc.astype(BF16)
    r = jax.nn.sigmoid(_gate_pre(xcb, wa_ref, ba_ref))
    i = jax.nn.sigmoid(_gate_pre(xcb, wx_ref, bx_ref))
    log_a = (-LRU_C) * r * jax.nn.softplus(-lam_ref[...])
    a = jnp.exp(log_a)
    u = jnp.sqrt(jnp.tanh(-log_a) * (1.0 + a * a)) * (i * xc)
    rec = lru_scan(a, u)
    y_gate = _dot(hb, win_ref[:, w:2 * w])
    out_lru = rec * jax.nn.gelu(y_gate)

    sc_b = _dot(hb, win_ref[:, 2 * w:2 * w + v])
    sc_c = _dot(hb, win_ref[:, 2 * w + v:2 * w + 2 * v])
    sc_x = _dot(hb, win_ref[:, 2 * w + 2 * v:2 * w + 3 * v])
    cx = sc_c * sc_x
    out_sc = sc_b * sc_conv(cx)

    n_l = _rms(out_lru, gl_ref[...]).astype(BF16)
    n_s = _rms(out_sc, gs_ref[...]).astype(BF16)
    mix = _dot(n_l, wout_ref[0:w, :]) + _dot(n_s, wout_ref[w:w + v, :])
    return x + gate1 * mix, x_lru, rec, cx


def _mixer_prompt_kernel(x_ref, mod_ref, n1g_ref, win_ref, cw_ref, cb_ref, wa_ref, ba_ref, wx_ref,
                         bx_ref, lam_ref, sw_ref, gl_ref, gs_ref, wout_ref,
                         x1_ref, hl_ref, lb_ref, sb_ref, *, tiles_per_seq):
    rows = x_ref.shape[0]
    hdr = SUBLANES

    @pl.when(pl.program_id(0) % tiles_per_seq == 0)
    def _():
        hl_ref[...] = jnp.zeros_like(hl_ref)
        lb_ref[...] = jnp.zeros_like(lb_ref)
        sb_ref[...] = jnp.zeros_like(sb_ref)

    def conv(cur, tail_ref, w_ref):
        width = w_ref.shape[0]
        ext = jnp.concatenate([tail_ref[...], cur], axis=0)
        out = w_ref[width - 1:width, :] * cur
        for d in range(1, width):
            out = out + w_ref[width - 1 - d:width - d, :] * pltpu.roll(ext, d, axis=0)[hdr:]
        return out

    new_x, x_lru, rec, cx = _mixer_core(
        x_ref[...], mod_ref, n1g_ref, win_ref, cb_ref, wa_ref, ba_ref, wx_ref, bx_ref, lam_ref,
        gl_ref, gs_ref, wout_ref,
        lru_conv=lambda cur: conv(cur, lb_ref, cw_ref),
        lru_scan=lambda a, u: _scan_rows(a, u, hl_ref[hdr - 1:hdr, :]),
        sc_conv=lambda cur: conv(cur, sb_ref, sw_ref))
    x1_ref[...] = new_x
    hl_ref[...] = rec[rows - hdr:, :]
    lb_ref[...] = x_lru[rows - hdr:, :]
    sb_ref[...] = cx[rows - hdr:, :]


def _mixer_sample_kernel(x_ref, mod_ref, n1g_ref, win_ref, cw_ref, cb_ref, wa_ref, ba_ref, wx_ref,
                         bx_ref, lam_ref, sw_ref, gl_ref, gs_ref, wout_ref, h0_ref, lb0_ref, sb0_ref,
                         x1_ref, h_ref, lb_ref, sb_ref):
    @pl.when(pl.program_id(0) == 0)
    def _():
        h_ref[...] = h0_ref[...]
        lb_ref[...] = lb0_ref[...]
        sb_ref[...] = sb0_ref[...]

    def conv(cur, hist_ref, w_ref):
        width = w_ref.shape[0]
        out = w_ref[width - 1:width, :] * cur
        for k in range(width - 1):
            out = out + w_ref[k:k + 1, :] * hist_ref[k]
        for k in range(width - 2):
            hist_ref[k] = hist_ref[k + 1]
        hist_ref[width - 2] = cur
        return out

    def scan(a, u):
        h = a * h_ref[...] + u
        h_ref[...] = h
        return h

    new_x, _, _, _ = _mixer_core(
        x_ref[...], mod_ref, n1g_ref, win_ref, cb_ref, wa_ref, ba_ref, wx_ref, bx_ref, lam_ref,
        gl_ref, gs_ref, wout_ref,
        lru_conv=lambda cur: conv(cur, lb_ref, cw_ref),
        lru_scan=scan,
        sc_conv=lambda cur: conv(cur, sb_ref, sw_ref))
    x1_ref[...] = new_x


def _mixer_weight_specs(lw):
    return [_const_spec(lw[k].shape) for k in
            ("norm1_g", "w_in", "lru_conv_w", "lru_conv_b", "wa_bd", "lru_ba", "wx_bd", "lru_bx",
             "lru_lambda", "sconv_w", "gnorm_lru_g", "gnorm_sc_g", "w_out")]


def _mixer_weight_args(lw):
    return [lw[k] for k in
            ("norm1_g", "w_in", "lru_conv_w", "lru_conv_b", "wa_bd", "lru_ba", "wx_bd", "lru_bx",
             "lru_lambda", "sconv_w", "gnorm_lru_g", "gnorm_sc_g", "w_out")]


def _mixer_prompt(x_flat, mod_p, lw, batch, seq):
    t, d = x_flat.shape
    w = lw["lru_conv_b"].shape[-1]
    v = lw["gnorm_sc_g"].shape[-1]
    r = PROMPT_TILE
    tps = seq // r
    state = lambda c: pl.BlockSpec((None, SUBLANES, c), lambda i: (i // tps, 0, 0))
    return pl.pallas_call(
        functools.partial(_mixer_prompt_kernel, tiles_per_seq=tps),
        grid=(t // r,),
        in_specs=[pl.BlockSpec((r, d), lambda i: (i, 0)),
                  pl.BlockSpec((6, None, 1, d), lambda i: (0, i // tps, 0, 0))] + _mixer_weight_specs(lw),
        out_specs=[pl.BlockSpec((r, d), lambda i: (i, 0)), state(w), state(w), state(v)],
        out_shape=[jax.ShapeDtypeStruct((t, d), F32),
                   jax.ShapeDtypeStruct((batch, SUBLANES, w), F32),
                   jax.ShapeDtypeStruct((batch, SUBLANES, w), F32),
                   jax.ShapeDtypeStruct((batch, SUBLANES, v), F32)],
        compiler_params=_params(("arbitrary",), 60),
    )(x_flat, mod_p, *_mixer_weight_args(lw))


def _mixer_sample(x_flat, mod_s, lw, h0, lb0, sb0, batch):
    t, d = x_flat.shape
    w = lw["lru_conv_b"].shape[-1]
    v = lw["gnorm_sc_g"].shape[-1]
    full = lambda shape: pl.BlockSpec(shape, lambda i: (0,) * len(shape))
    return pl.pallas_call(
        _mixer_sample_kernel,
        grid=(t // batch,),
        in_specs=[pl.BlockSpec((batch, d), lambda i: (i, 0)), _const_spec(mod_s.shape)]
        + _mixer_weight_specs(lw) + [_const_spec(h0.shape), _const_spec(lb0.shape), _const_spec(sb0.shape)],
        out_specs=[pl.BlockSpec((batch, d), lambda i: (i, 0)), full(h0.shape), full(lb0.shape), full(sb0.shape)],
        out_shape=[jax.ShapeDtypeStruct((t, d), F32),
                   jax.ShapeDtypeStruct(h0.shape, F32),
                   jax.ShapeDtypeStruct(lb0.shape, F32),
                   jax.ShapeDtypeStruct(sb0.shape, F32)],
        compiler_params=_params(("arbitrary",), 60),
    )(x_flat, mod_s, *_mixer_weight_args(lw), h0, lb0, sb0)


def _oddeven_merge_sort_pairs(n):
    def merge(lo, hi, r):
        step = r * 2
        if step < hi - lo:
            yield from merge(lo, hi, step)
            yield from merge(lo + r, hi, step)
            yield from ((i, i + r) for i in range(lo + r, hi - r, step))
        else:
            yield (lo, lo + r)

    def sort(lo, hi):
        if hi - lo >= 1:
            mid = lo + (hi - lo) // 2
            yield from sort(lo, mid)
            yield from sort(mid + 1, hi)
            yield from merge(lo, hi, 1)

    return list(sort(0, n - 1))


def _top_values(work, k):
    n, r = work.shape
    assert k == 2 * SUBLANES and n <= k * SUBLANES and n % SUBLANES == 0
    w = [work[v * SUBLANES:(v + 1) * SUBLANES, :] for v in range(n // SUBLANES)]
    w += [jnp.full((SUBLANES, r), -jnp.inf, F32)] * (k - len(w))
    for i, j in _oddeven_merge_sort_pairs(k):
        w[i], w[j] = jnp.maximum(w[i], w[j]), jnp.minimum(w[i], w[j])
    shift = SUBLANES // 2
    while shift >= 1:
        w = [jnp.maximum(w[v], pltpu.roll(w[k - 1 - v], shift, axis=0)) for v in range(k)]
        dist = k // 2
        while dist >= 1:
            for v in range(k):
                if v & dist == 0:
                    w[v], w[v + dist] = jnp.maximum(w[v], w[v + dist]), jnp.minimum(w[v], w[v + dist])
            dist //= 2
        shift //= 2
    row = lax.broadcasted_iota(jnp.int32, (SUBLANES, 1), 0)
    halves = []
    for h in range(k // SUBLANES):
        out = w[h * SUBLANES]
        for v in range(1, SUBLANES):
            out = jnp.where(row == v, w[h * SUBLANES + v], out)
        halves.append(out)
    return jnp.concatenate(halves, axis=0)


def _prep_kernel(x1_ref, mod_ref, n2g_ref, wq_ref, sk_ref, h2t_ref, kc_ref, e1_ref, r2_ref, e2_ref):
    nk = sk_ref.shape[1]
    n_heads = wq_ref.shape[1] // (2 * nk)
    k = PEER_TOPK
    shift2, scale2 = mod_ref[3], mod_ref[4]
    h2 = _rms(x1_ref[...], n2g_ref[...]) * (1.0 + scale2) + shift2
    h2t_ref[...] = h2.T.astype(BF16)
    qb = _dot(h2.astype(BF16), wq_ref[...]).astype(BF16)
    nt = (((1,), (1,)), ((), ()))
    for hd in range(n_heads):
        s1, s2 = [lax.dot_general(sk_ref[p], qb[:, (2 * hd + p) * nk:(2 * hd + p + 1) * nk], nt,
                                  preferred_element_type=F32) for p in range(2)]
        v1 = _top_values(s1, k)
        v2 = _top_values(s2, k)
        rank2 = jnp.full(s2.shape, float(nk), F32)
        for b in range(k):
            rank2 = jnp.where(s2 == v2[b:b + 1, :], float(b), rank2)
        blocks = [v1[0:1, :] + v2]
        for a in range(1, SUBLANES):
            blocks.append(v1[a:a + 1, :] + v2[0:SUBLANES, :])
        blocks.append(v1[SUBLANES:, :] + v2[0:1, :])
        top = _top_values(jnp.concatenate(blocks, axis=0), k)
        z = jnp.sum(jnp.exp(top - top[0:1, :]), axis=0, keepdims=True)
        tau = top[k - 1:k, :]
        count_a = jnp.zeros_like(v1)
        for b in range(k):
            count_a = count_a + jnp.where(v1 + v2[b:b + 1, :] >= tau, 1.0, 0.0)
        count = jnp.zeros_like(s1)
        for a in range(k):
            count = jnp.where(s1 == v1[a:a + 1, :], count_a[a:a + 1, :], count)
        kc_ref[hd] = count
        e1_ref[hd] = jnp.exp(s1 - v1[0:1, :]) * (1.0 / z)
        r2_ref[hd * nk:(hd + 1) * nk, :] = rank2.astype(BF16)
        e2_ref[hd * nk:(hd + 1) * nk, :] = jnp.exp(s2 - v2[0:1, :]).astype(BF16)


def _peer_prep(x1_flat, mod, mod_spec, rows, lw):
    t, d = x1_flat.shape
    nq = lw["peer_wq"].shape[1]
    nk = lw["peer_sub_keys"].shape[1]
    n_heads = nq // (2 * nk)
    return pl.pallas_call(
        _prep_kernel,
        grid=(t // rows,),
        in_specs=[pl.BlockSpec((rows, d), lambda i: (i, 0)), mod_spec,
                  _const_spec(lw["norm2_g"].shape), _const_spec(lw["peer_wq"].shape),
                  _const_spec(lw["peer_sub_keys"].shape)],
        out_specs=[pl.BlockSpec((d, rows), lambda i: (0, i)),
                   pl.BlockSpec((n_heads, nk, rows), lambda i: (0, 0, i)),
                   pl.BlockSpec((n_heads, nk, rows), lambda i: (0, 0, i)),
                   pl.BlockSpec((n_heads * nk, rows), lambda i: (0, i)),
                   pl.BlockSpec((n_heads * nk, rows), lambda i: (0, i))],
        out_shape=[jax.ShapeDtypeStruct((d, t), BF16),
                   jax.ShapeDtypeStruct((n_heads, nk, t), F32),
                   jax.ShapeDtypeStruct((n_heads, nk, t), F32),
                   jax.ShapeDtypeStruct((n_heads * nk, t), BF16),
                   jax.ShapeDtypeStruct((n_heads * nk, t), BF16)],
        compiler_params=_params(("arbitrary",), 48),
    )(x1_flat, mod, lw["norm2_g"], lw["peer_wq"], lw["peer_sub_keys"])


def _peer_kernel(h2t_ref, kc_ref, e1_ref, r2_ref, e2_ref, u_ref, vt_ref, x1_ref, g2_ref, fg_ref, y_ref,
                 bc_ref, s0_ref, s1_ref, a_ref, out_ref, *, n_heads, nk, n_chunks, final_norm):
    g = pl.program_id(0)
    c_gate = jnp.maximum(g - 1, 0) % n_chunks
    ec, tb = s0_ref.shape
    i_group = bc_ref.shape[0]
    s_rows = i_group * nk
    jrows = bc_ref.shape[2]
    gl = min(PEER_GATE_LANES, tb)

    @pl.when(g == 0)
    def _():
        s1_ref[...] = jnp.zeros_like(s1_ref)

    @pl.when(c_gate == 0)
    def _():
        out_ref[...] = jnp.zeros_like(out_ref)

    def run(s_w, s_r):
        def gate_piece(r1, jq, l0):
            tiles = [jq * PEER_J_TILES + jt for jt in range(PEER_J_TILES)]
            gates = [[jnp.zeros((jrows, gl), BF16) for _ in tiles] for _ in range(i_group)]
            for hd in range(n_heads):
                kcb = [bc_ref[ii, 2 * hd, :, l0:l0 + gl] for ii in range(i_group)]
                e1b = [bc_ref[ii, 2 * hd + 1, :, l0:l0 + gl] for ii in range(i_group)]
                for n, jt in enumerate(tiles):
                    j0 = hd * nk + jt * jrows
                    r2 = r2_ref[j0:j0 + jrows, l0:l0 + gl]
                    e2 = e2_ref[j0:j0 + jrows, l0:l0 + gl]
                    for ii in range(i_group):
                        gates[ii][n] = gates[ii][n] + jnp.where(r2 < kcb[ii], e2 * e1b[ii], 0.0)
            for ii in range(i_group):
                for n, jt in enumerate(tiles):
                    row = r1 + ii * nk + jt * jrows
                    act = jax.nn.gelu(s_r[row:row + jrows, l0:l0 + gl]) * gates[ii][n]
                    a_ref[row:row + jrows, l0:l0 + gl] = act

        s_w[...] = _dot(u_ref[...], h2t_ref[...]).astype(BF16)
        for it in range(ec // s_rows):
            r1 = it * s_rows
            for ii in range(i_group):
                il = it * i_group + ii
                for hd in range(n_heads):
                    bc_ref[ii, 2 * hd] = jnp.broadcast_to(kc_ref[hd, il:il + 1, :].astype(BF16), (jrows, tb))
                    bc_ref[ii, 2 * hd + 1] = jnp.broadcast_to(e1_ref[hd, il:il + 1, :].astype(BF16), (jrows, tb))
            for jq, l0 in itertools.product(range(nk // (jrows * PEER_J_TILES)), range(0, tb, gl)):
                gate_piece(r1, jq, l0)
            out_ref[...] += _dot(vt_ref[:, r1:r1 + s_rows], a_ref[r1:r1 + s_rows, :])

    @pl.when(g % 2 == 0)
    def _():
        run(s0_ref, s1_ref)

    @pl.when(g % 2 == 1)
    def _():
        run(s1_ref, s0_ref)

    @pl.when((g > 0) & (c_gate == n_chunks - 1))
    def _():
        rows = g2_ref.shape[0]
        r_n = min(max(rows, MXU_DIM), tb)
        for r0 in range(0, tb, r_n):
            peer = out_ref[:, r0:r0 + r_n].T
            gate2 = g2_ref[...] if rows in (1, r_n) else jnp.concatenate([g2_ref[...]] * (r_n // rows), axis=0)
            xo = x1_ref[r0:r0 + r_n, :] + gate2 * peer
            y_ref[r0:r0 + r_n, :] = _rms(xo, fg_ref[...]) if final_norm else xo


def _peer_experts(h2t, kc, e1, r2, e2, u_bf, vt_bf, x1_flat, gate2, gate2_spec, final_g, final_norm):
    d, t = h2t.shape
    n_heads, nk, _ = kc.shape
    ne = u_bf.shape[0]
    tb = min(PEER_TOKENS, t)
    ec = PEER_EXPERTS
    nc = ne // ec
    n = (t // tb) * nc
    gate_step = lambda g: jnp.maximum(g - 1, 0)
    chunk_tab = pl.BlockSpec((n_heads, ec // nk, tb), lambda g: (0, gate_step(g) % nc, gate_step(g) // nc))
    token_tab = pl.BlockSpec((n_heads * nk, tb), lambda g: (0, gate_step(g) // nc),
                             pipeline_mode=pl.Buffered(1))
    return pl.pallas_call(
        functools.partial(_peer_kernel, n_heads=n_heads, nk=nk, n_chunks=nc, final_norm=final_norm),
        grid=(n + 1,),
        in_specs=[pl.BlockSpec((d, tb), lambda g: (0, jnp.minimum(g, n - 1) // nc),
                               pipeline_mode=pl.Buffered(1)),
                  chunk_tab, chunk_tab, token_tab, token_tab,
                  pl.BlockSpec((ec, d), lambda g: (jnp.minimum(g, n - 1) % nc, 0)),
                  pl.BlockSpec((None, d, ec), lambda g: (gate_step(g) % nc, 0, 0)),
                  pl.BlockSpec((tb, d), lambda g: (gate_step(g) // nc, 0), pipeline_mode=pl.Buffered(1)),
                  gate2_spec(lambda g: gate_step(g) // nc), _const_spec(final_g.shape)],
        out_specs=pl.BlockSpec((tb, d), lambda g: (gate_step(g) // nc, 0)),
        out_shape=jax.ShapeDtypeStruct((t, d), F32),
        scratch_shapes=[pltpu.VMEM((PEER_I_GROUP, 2 * n_heads, 2 * SUBLANES, tb), BF16),
                        pltpu.VMEM((ec, tb), BF16),
                        pltpu.VMEM((ec, tb), BF16),
                        pltpu.VMEM((ec, tb), BF16),
                        pltpu.VMEM((d, tb), F32)],
        compiler_params=_params(("arbitrary",), 56),
    )(h2t, kc, e1, r2, e2, u_bf, vt_bf, x1_flat, gate2, final_g)


def _block_diag(w):
    heads, hd, _ = w.shape
    per = MXU_DIM // hd
    wg = w.reshape(heads // per, per, hd, hd)
    eye = jnp.eye(per, dtype=w.dtype)
    return jnp.einsum("gpij,pq->gpiqj", wg, eye).reshape(heads // per, MXU_DIM, MXU_DIM).astype(BF16)


def kernel(x_prompt, x_sample, c_prompt, c_sample, state_lru_h, state_lru_conv, state_sconv, w_ada, b_ada, norm1_g, norm2_g, w_in, lru_conv_w, lru_conv_b, lru_wa, lru_ba, lru_wx, lru_bx, lru_lambda, sconv_w, gnorm_lru_g, gnorm_sc_g, w_out, peer_wq, peer_sub_keys, peer_u, peer_v, final_g):
    bp, seq, d = x_prompt.shape
    bs, sseq, _ = x_sample.shape
    depth = w_ada.shape[0]

    xp = x_prompt.reshape(bp * seq, d)
    xs = jnp.swapaxes(x_sample, 0, 1).reshape(sseq * bs, d)
    n_c = bp + bs
    pad = (-n_c) % SUBLANES
    c_all = jnp.concatenate([c_prompt, c_sample, jnp.zeros((pad, d), F32)], axis=0)
    fg = final_g.reshape(1, d)

    mod_p_spec = pl.BlockSpec((6, None, 1, d), lambda i: (0, i // (seq // PROMPT_TILE), 0, 0))
    mod_s_spec = _const_spec((6, bs, d))
    blocks_per_seq = seq // PEER_TOKENS
    gate2_p_spec = lambda blk: pl.BlockSpec((None, None, 1, d), lambda g: (5, blk(g) // blocks_per_seq, 0, 0))
    gate2_s_spec = lambda blk: pl.BlockSpec((None, bs, d), lambda g: (5, 0, 0), pipeline_mode=pl.Buffered(1))

    outs = [[] for _ in range(6)]
    for l in range(depth):
        row = lambda a: a[l].reshape(1, -1)
        lw = dict(
            norm1_g=row(norm1_g), norm2_g=row(norm2_g), w_in=w_in[l].astype(BF16),
            lru_conv_w=lru_conv_w[l], lru_conv_b=row(lru_conv_b),
            wa_bd=_block_diag(lru_wa[l]), lru_ba=row(lru_ba),
            wx_bd=_block_diag(lru_wx[l]), lru_bx=row(lru_bx), lru_lambda=row(lru_lambda),
            sconv_w=sconv_w[l], gnorm_lru_g=row(gnorm_lru_g), gnorm_sc_g=row(gnorm_sc_g),
            w_out=w_out[l].astype(BF16), peer_wq=peer_wq[l].astype(BF16),
            peer_sub_keys=peer_sub_keys[l].astype(BF16))
        u_bf = peer_u[l].astype(BF16)
        ne = peer_v.shape[1]
        vt_bf = jnp.swapaxes(peer_v[l].reshape(ne // PEER_EXPERTS, PEER_EXPERTS, d), 1, 2).astype(BF16)

        mod = _adaln(c_all, w_ada[l], b_ada[l])
        mod_p = jnp.swapaxes(mod[:bp].reshape(bp, 6, d), 0, 1).reshape(6, bp, 1, d)
        mod_s = jnp.swapaxes(mod[bp:n_c].reshape(bs, 6, d), 0, 1)

        xp1, hl, lb, sb = _mixer_prompt(xp, mod_p, lw, bp, seq)
        h0 = state_lru_h[l]
        lb0 = jnp.swapaxes(state_lru_conv[l], 0, 1)
        sb0 = jnp.swapaxes(state_sconv[l], 0, 1)
        xs1, hs, lbs, sbs = _mixer_sample(xs, mod_s, lw, h0, lb0, sb0, bs)

        last = l == depth - 1
        new_x = []
        for x1, mod_g, spec, rows, g2_spec in ((xp1, mod_p, mod_p_spec, PROMPT_TILE, gate2_p_spec),
                                               (xs1, mod_s, mod_s_spec, bs, gate2_s_spec)):
            new_x.append(_peer_experts(*_peer_prep(x1, mod_g, spec, rows, lw), u_bf, vt_bf,
                                       x1, mod_g, g2_spec, fg, last))
        xp, xs = new_x

        nl = lru_conv_w.shape[1] - 1
        ns = sconv_w.shape[1] - 1
        outs[0].append(hl[:, SUBLANES - 1, :])
        outs[1].append(lb[:, SUBLANES - nl:, :])
        outs[2].append(sb[:, SUBLANES - ns:, :])
        outs[3].append(hs)
        outs[4].append(jnp.swapaxes(lbs, 0, 1))
        outs[5].append(jnp.swapaxes(sbs, 0, 1))

    y_prompt = xp.reshape(bp, seq, d)
    y_sample = jnp.swapaxes(xs.reshape(sseq, bs, d), 0, 1)
    return (y_prompt, y_sample) + tuple(jnp.stack(o) for o in outs)
```

```python
import functools
import itertools

import jax
import jax.numpy as jnp
from jax import lax
from jax.experimental import pallas as pl
from jax.experimental.pallas import tpu as pltpu

F32 = jnp.float32
BF16 = jnp.bfloat16

EPS = 1e-6
LRU_C = 8.0
PEER_TOPK = 16

SUBLANES = 8
MXU_DIM = 256

PROMPT_TILE = 256
PREP_TILE = 512
PEER_TOKENS = 512
PEER_EXPERTS = 1024
PEER_I_GROUP = 2
PEER_GATE_LANES = 256
PEER_J_TILES = 4
ADALN_COLS = 1024


def _dot(a, b):
    return jnp.dot(a, b, preferred_element_type=F32)


def _rms(x, g):
    ms = jnp.mean(x * x, axis=-1, keepdims=True)
    return x * lax.rsqrt(ms + EPS) * g


def _const_spec(shape):
    nd = len(shape)
    return pl.BlockSpec(shape, lambda *_: (0,) * nd, pipeline_mode=pl.Buffered(1))


def _params(sem, vmem_mb):
    return pltpu.CompilerParams(dimension_semantics=sem, vmem_limit_bytes=vmem_mb * 1024 * 1024)


def _adaln_kernel(c_ref, w_ref, b_ref, o_ref):
    sc = jax.nn.silu(c_ref[...]).astype(BF16)
    o_ref[...] = _dot(sc, w_ref[...].astype(BF16)) + b_ref[...]


def _adaln(c_all, w_ada, b_ada):
    m, d = c_all.shape
    n = w_ada.shape[1]
    tn = ADALN_COLS
    return pl.pallas_call(
        _adaln_kernel,
        grid=(n // tn,),
        in_specs=[pl.BlockSpec((m, d), lambda j: (0, 0)),
                  pl.BlockSpec((d, tn), lambda j: (0, j)),
                  pl.BlockSpec((1, tn), lambda j: (0, j))],
        out_specs=pl.BlockSpec((m, tn), lambda j: (0, j)),
        out_shape=jax.ShapeDtypeStruct((m, n), F32),
        compiler_params=_params(("arbitrary",), 40),
    )(c_all, w_ada, b_ada.reshape(1, n))


def _gate_pre(xcb, w_ref, b_ref):
    nblk = w_ref.shape[0]
    parts = [_dot(xcb[:, g * MXU_DIM:(g + 1) * MXU_DIM], w_ref[g]) for g in range(nblk)]
    return jnp.concatenate(parts, axis=1) + b_ref[...]


def _scan_rows(a, u, h_in):
    rows = a.shape[0]
    row = lax.broadcasted_iota(jnp.int32, (rows, 1), 0)
    acc_a = a
    acc_b = u + jnp.where(row == 0, a * h_in, 0.0)
    d = 1
    while d < rows:
        keep = row >= d
        sh_b = pltpu.roll(acc_b, d, axis=0)
        acc_b = acc_b + jnp.where(keep, acc_a * sh_b, 0.0)
        if 2 * d < rows:
            sh_a = pltpu.roll(acc_a, d, axis=0)
            acc_a = jnp.where(keep, acc_a * sh_a, acc_a)
        d *= 2
    return acc_b


def _mixer_core(x, mod_ref, n1g_ref, win_ref, cb_ref, wa_ref, ba_ref, wx_ref, bx_ref, lam_ref,
                gl_ref, gs_ref, wout_ref, lru_conv, lru_scan, sc_conv):
    w = cb_ref.shape[-1]
    v = gs_ref.shape[-1]
    shift1, scale1, gate1 = mod_ref[0], mod_ref[1], mod_ref[2]
    h = _rms(x, n1g_ref[...]) * (1.0 + scale1) + shift1
    hb = h.astype(BF16)

    x_lru = _dot(hb, win_ref[:, 0:w])
    xc = lru_conv(x_lru) + cb_ref[...]
    xcb = xc.astype(BF16)
    r = jax.nn.sigmoid(_gate_pre(xcb, wa_ref, ba_ref))
    i = jax.nn.sigmoid(_gate_pre(xcb, wx_ref, bx_ref))
    log_a = (-LRU_C) * r * jax.nn.softplus(-lam_ref[...])
    a = jnp.exp(log_a)
    u = jnp.sqrt(jnp.tanh(-log_a) * (1.0 + a * a)) * (i * xc)
    rec = lru_scan(a, u)
    y_gate = _dot(hb, win_ref[:, w:2 * w])
    out_lru = rec * jax.nn.gelu(y_gate)

    sc_b = _dot(hb, win_ref[:, 2 * w:2 * w + v])
    sc_c = _dot(hb, win_ref[:, 2 * w + v:2 * w + 2 * v])
    sc_x = _dot(hb, win_ref[:, 2 * w + 2 * v:2 * w + 3 * v])
    cx = sc_c * sc_x
    out_sc = sc_b * sc_conv(cx)

    n_l = _rms(out_lru, gl_ref[...]).astype(BF16)
    n_s = _rms(out_sc, gs_ref[...]).astype(BF16)
    mix = _dot(n_l, wout_ref[0:w, :]) + _dot(n_s, wout_ref[w:w + v, :])
    return x + gate1 * mix, x_lru, rec, cx


def _mixer_prompt_kernel(x_ref, mod_ref, n1g_ref, win_ref, cw_ref, cb_ref, wa_ref, ba_ref, wx_ref,
                         bx_ref, lam_ref, sw_ref, gl_ref, gs_ref, wout_ref,
                         x1_ref, hl_ref, lb_ref, sb_ref, *, tiles_per_seq):
    rows = x_ref.shape[0]
    hdr = SUBLANES

    @pl.when(pl.program_id(0) % tiles_per_seq == 0)
    def _():
        hl_ref[...] = jnp.zeros_like(hl_ref)
        lb_ref[...] = jnp.zeros_like(lb_ref)
        sb_ref[...] = jnp.zeros_like(sb_ref)

    def conv(cur, tail_ref, w_ref):
        width = w_ref.shape[0]
        ext = jnp.concatenate([tail_ref[...], cur], axis=0)
        out = w_ref[width - 1:width, :] * cur
        for d in range(1, width):
            out = out + w_ref[width - 1 - d:width - d, :] * pltpu.roll(ext, d, axis=0)[hdr:]
        return out

    new_x, x_lru, rec, cx = _mixer_core(
        x_ref[...], mod_ref, n1g_ref, win_ref, cb_ref, wa_ref, ba_ref, wx_ref, bx_ref, lam_ref,
        gl_ref, gs_ref, wout_ref,
        lru_conv=lambda cur: conv(cur, lb_ref, cw_ref),
        lru_scan=lambda a, u: _scan_rows(a, u, hl_ref[hdr - 1:hdr, :]),
        sc_conv=lambda cur: conv(cur, sb_ref, sw_ref))
    x1_ref[...] = new_x
    hl_ref[...] = rec[rows - hdr:, :]
    lb_ref[...] = x_lru[rows - hdr:, :]
    sb_ref[...] = cx[rows - hdr:, :]


def _mixer_sample_kernel(x_ref, mod_ref, n1g_ref, win_ref, cw_ref, cb_ref, wa_ref, ba_ref, wx_ref,
                         bx_ref, lam_ref, sw_ref, gl_ref, gs_ref, wout_ref, h0_ref, lb0_ref, sb0_ref,
                         x1_ref, h_ref, lb_ref, sb_ref):
    @pl.when(pl.program_id(0) == 0)
    def _():
        h_ref[...] = h0_ref[...]
        lb_ref[...] = lb0_ref[...]
        sb_ref[...] = sb0_ref[...]

    def conv(cur, hist_ref, w_ref):
        width = w_ref.shape[0]
        out = w_ref[width - 1:width, :] * cur
        for k in range(width - 1):
            out = out + w_ref[k:k + 1, :] * hist_ref[k]
        for k in range(width - 2):
            hist_ref[k] = hist_ref[k + 1]
        hist_ref[width - 2] = cur
        return out

    def scan(a, u):
        h = a * h_ref[...] + u
        h_ref[...] = h
        return h

    new_x, _, _, _ = _mixer_core(
        x_ref[...], mod_ref, n1g_ref, win_ref, cb_ref, wa_ref, ba_ref, wx_ref, bx_ref, lam_ref,
        gl_ref, gs_ref, wout_ref,
        lru_conv=lambda cur: conv(cur, lb_ref, cw_ref),
        lru_scan=scan,
        sc_conv=lambda cur: conv(cur, sb_ref, sw_ref))
    x1_ref[...] = new_x


def _mixer_weight_specs(lw):
    return [_const_spec(lw[k].shape) for k in
            ("norm1_g", "w_in", "lru_conv_w", "lru_conv_b", "wa_bd", "lru_ba", "wx_bd", "lru_bx",
             "lru_lambda", "sconv_w", "gnorm_lru_g", "gnorm_sc_g", "w_out")]


def _mixer_weight_args(lw):
    return [lw[k] for k in
            ("norm1_g", "w_in", "lru_conv_w", "lru_conv_b", "wa_bd", "lru_ba", "wx_bd", "lru_bx",
             "lru_lambda", "sconv_w", "gnorm_lru_g", "gnorm_sc_g", "w_out")]


def _mixer_prompt(x_flat, mod_p, lw, batch, seq):
    t, d = x_flat.shape
    w = lw["lru_conv_b"].shape[-1]
    v = lw["gnorm_sc_g"].shape[-1]
    r = PROMPT_TILE
    tps = seq // r
    state = lambda c: pl.BlockSpec((None, SUBLANES, c), lambda i: (i // tps, 0, 0))
    return pl.pallas_call(
        functools.partial(_mixer_prompt_kernel, tiles_per_seq=tps),
        grid=(t // r,),
        in_specs=[pl.BlockSpec((r, d), lambda i: (i, 0)),
                  pl.BlockSpec((6, None, 1, d), lambda i: (0, i // tps, 0, 0))] + _mixer_weight_specs(lw),
        out_specs=[pl.BlockSpec((r, d), lambda i: (i, 0)), state(w), state(w), state(v)],
        out_shape=[jax.ShapeDtypeStruct((t, d), F32),
                   jax.ShapeDtypeStruct((batch, SUBLANES, w), F32),
                   jax.ShapeDtypeStruct((batch, SUBLANES, w), F32),
                   jax.ShapeDtypeStruct((batch, SUBLANES, v), F32)],
        compiler_params=_params(("arbitrary",), 60),
    )(x_flat, mod_p, *_mixer_weight_args(lw))


def _mixer_sample(x_flat, mod_s, lw, h0, lb0, sb0, batch):
    t, d = x_flat.shape
    w = lw["lru_conv_b"].shape[-1]
    v = lw["gnorm_sc_g"].shape[-1]
    full = lambda shape: pl.BlockSpec(shape, lambda i: (0,) * len(shape))
    return pl.pallas_call(
        _mixer_sample_kernel,
        grid=(t // batch,),
        in_specs=[pl.BlockSpec((batch, d), lambda i: (i, 0)), _const_spec(mod_s.shape)]
        + _mixer_weight_specs(lw) + [_const_spec(h0.shape), _const_spec(lb0.shape), _const_spec(sb0.shape)],
        out_specs=[pl.BlockSpec((batch, d), lambda i: (i, 0)), full(h0.shape), full(lb0.shape), full(sb0.shape)],
        out_shape=[jax.ShapeDtypeStruct((t, d), F32),
                   jax.ShapeDtypeStruct(h0.shape, F32),
                   jax.ShapeDtypeStruct(lb0.shape, F32),
                   jax.ShapeDtypeStruct(sb0.shape, F32)],
        compiler_params=_params(("arbitrary",), 60),
    )(x_flat, mod_s, *_mixer_weight_args(lw), h0, lb0, sb0)


def _oddeven_merge_sort_pairs(n):
    def merge(lo, hi, r):
        step = r * 2
        if step < hi - lo:
            yield from merge(lo, hi, step)
            yield from merge(lo + r, hi, step)
            yield from ((i, i + r) for i in range(lo + r, hi - r, step))
        else:
            yield (lo, lo + r)

    def sort(lo, hi):
        if hi - lo >= 1:
            mid = lo + (hi - lo) // 2
            yield from sort(lo, mid)
            yield from sort(mid + 1, hi)
            yield from merge(lo, hi, 1)

    return list(sort(0, n - 1))


def _top_values(work, k):
    n, r = work.shape
    assert k == 2 * SUBLANES and n <= k * SUBLANES and n % SUBLANES == 0
    w = [work[v * SUBLANES:(v + 1) * SUBLANES, :] for v in range(n // SUBLANES)]
    w += [jnp.full((SUBLANES, r), -jnp.inf, F32)] * (k - len(w))
    for i, j in _oddeven_merge_sort_pairs(k):
        w[i], w[j] = jnp.maximum(w[i], w[j]), jnp.minimum(w[i], w[j])
    shift = SUBLANES // 2
    while shift >= 1:
        w = [jnp.maximum(w[v], pltpu.roll(w[k - 1 - v], shift, axis=0)) for v in range(k)]
        dist = k // 2
        while dist >= 1:
            for v in range(k):
                if v & dist == 0:
                    w[v], w[v + dist] = jnp.maximum(w[v], w[v + dist]), jnp.minimum(w[v], w[v + dist])
            dist //= 2
        shift //= 2
    row = lax.broadcasted_iota(jnp.int32, (SUBLANES, 1), 0)
    halves = []
    for h in range(k // SUBLANES):
        out = w[h * SUBLANES]
        for v in range(1, SUBLANES):
            out = jnp.where(row == v, w[h * SUBLANES + v], out)
        halves.append(out)
    return jnp.concatenate(halves, axis=0)


def _prep_kernel(x1_ref, mod_ref, n2g_ref, wq_ref, sk_ref, h2t_ref, kc_ref, e1_ref, r2_ref, e2_ref):
    nk = sk_ref.shape[1]
    n_heads = wq_ref.shape[1] // (2 * nk)
    k = PEER_TOPK
    shift2, scale2 = mod_ref[3], mod_ref[4]
    h2 = _rms(x1_ref[...], n2g_ref[...]) * (1.0 + scale2) + shift2
    h2t_ref[...] = h2.T.astype(BF16)
    qb = _dot(h2.astype(BF16), wq_ref[...]).astype(BF16)
    nt = (((1,), (1,)), ((), ()))
    for hd in range(n_heads):
        s1, s2 = [lax.dot_general(sk_ref[p], qb[:, (2 * hd + p) * nk:(2 * hd + p + 1) * nk], nt,
                                  preferred_element_type=F32) for p in range(2)]
        v1 = _top_values(s1, k)
        v2 = _top_values(s2, k)
        rank2 = jnp.full(s2.shape, float(nk), F32)
        for b in range(k):
            rank2 = jnp.where(s2 == v2[b:b + 1, :], float(b), rank2)
        blocks = [v1[0:1, :] + v2]
        for a in range(1, SUBLANES):
            blocks.append(v1[a:a + 1, :] + v2[0:SUBLANES, :])
        blocks.append(v1[SUBLANES:, :] + v2[0:1, :])
        top = _top_values(jnp.concatenate(blocks, axis=0), k)
        z = jnp.sum(jnp.exp(top - top[0:1, :]), axis=0, keepdims=True)
        tau = top[k - 1:k, :]
        count_a = jnp.zeros_like(v1)
        for b in range(k):
            count_a = count_a + jnp.where(v1 + v2[b:b + 1, :] >= tau, 1.0, 0.0)
        count = jnp.zeros_like(s1)
        for a in range(k):
            count = jnp.where(s1 == v1[a:a + 1, :], count_a[a:a + 1, :], count)
        kc_ref[hd] = count
        e1_ref[hd] = jnp.exp(s1 - v1[0:1, :]) * (1.0 / z)
        r2_ref[hd * nk:(hd + 1) * nk, :] = rank2.astype(BF16)
        e2_ref[hd * nk:(hd + 1) * nk, :] = jnp.exp(s2 - v2[0:1, :]).astype(BF16)


def _peer_prep(x1_flat, mod, mod_spec, rows, lw):
    t, d = x1_flat.shape
    nq = lw["peer_wq"].shape[1]
    nk = lw["peer_sub_keys"].shape[1]
    n_heads = nq // (2 * nk)
    return pl.pallas_call(
        _prep_kernel,
        grid=(t // rows,),
        in_specs=[pl.BlockSpec((rows, d), lambda i: (i, 0)), mod_spec,
                  _const_spec(lw["norm2_g"].shape), _const_spec(lw["peer_wq"].shape),
                  _const_spec(lw["peer_sub_keys"].shape)],
        out_specs=[pl.BlockSpec((d, rows), lambda i: (0, i)),
                   pl.BlockSpec((n_heads, nk, rows), lambda i: (0, 0, i)),
                   pl.BlockSpec((n_heads, nk, rows), lambda i: (0, 0, i)),
                   pl.BlockSpec((n_heads * nk, rows), lambda i: (0, i)),
                   pl.BlockSpec((n_heads * nk, rows), lambda i: (0, i))],
        out_shape=[jax.ShapeDtypeStruct((d, t), BF16),
                   jax.ShapeDtypeStruct((n_heads, nk, t), F32),
                   jax.ShapeDtypeStruct((n_heads, nk, t), F32),
                   jax.ShapeDtypeStruct((n_heads * nk, t), BF16),
                   jax.ShapeDtypeStruct((n_heads * nk, t), BF16)],
        compiler_params=_params(("arbitrary",), 48),
    )(x1_flat, mod, lw["norm2_g"], lw["peer_wq"], lw["peer_sub_keys"])


def _peer_kernel(h2t_ref, kc_ref, e1_ref, r2_ref, e2_ref, u_ref, vt_ref, x1_ref, g2_ref, fg_ref, y_ref,
                 bc_ref, s0_ref, s1_ref, a_ref, out_ref, *, n_heads, nk, n_chunks, final_norm):
    g = pl.program_id(0)
    c_gate = jnp.maximum(g - 1, 0) % n_chunks
    ec, tb = s0_ref.shape
    i_group = bc_ref.shape[0]
    s_rows = i_group * nk
    jrows = bc_ref.shape[2]
    gl = min(PEER_GATE_LANES, tb)

    @pl.when(g == 0)
    def _():
        s1_ref[...] = jnp.zeros_like(s1_ref)

    @pl.when(c_gate == 0)
    def _():
        out_ref[...] = jnp.zeros_like(out_ref)

    def run(s_w, s_r):
        def gate_piece(r1, jq, l0):
            tiles = [jq * PEER_J_TILES + jt for jt in range(PEER_J_TILES)]
            gates = [[jnp.zeros((jrows, gl), BF16) for _ in tiles] for _ in range(i_group)]
            for hd in range(n_heads):
                kcb = [bc_ref[ii, 2 * hd, :, l0:l0 + gl] for ii in range(i_group)]
                e1b = [bc_ref[ii, 2 * hd + 1, :, l0:l0 + gl] for ii in range(i_group)]
                for n, jt in enumerate(tiles):
                    j0 = hd * nk + jt * jrows
                    r2 = r2_ref[j0:j0 + jrows, l0:l0 + gl]
                    e2 = e2_ref[j0:j0 + jrows, l0:l0 + gl]
                    for ii in range(i_group):
                        gates[ii][n] = gates[ii][n] + jnp.where(r2 < kcb[ii], e2 * e1b[ii], 0.0)
            for ii in range(i_group):
                for n, jt in enumerate(tiles):
                    row = r1 + ii * nk + jt * jrows
                    act = jax.nn.gelu(s_r[row:row + jrows, l0:l0 + gl]) * gates[ii][n]
                    a_ref[row:row + jrows, l0:l0 + gl] = act

        s_w[...] = _dot(u_ref[...], h2t_ref[...]).astype(BF16)
        for it in range(ec // s_rows):
            r1 = it * s_rows
            for ii in range(i_group):
                il = it * i_group + ii
                for hd in range(n_heads):
                    bc_ref[ii, 2 * hd] = jnp.broadcast_to(kc_ref[hd, il:il + 1, :].astype(BF16), (jrows, tb))
                    bc_ref[ii, 2 * hd + 1] = jnp.broadcast_to(e1_ref[hd, il:il + 1, :].astype(BF16), (jrows, tb))
            for jq, l0 in itertools.product(range(nk // (jrows * PEER_J_TILES)), range(0, tb, gl)):
                gate_piece(r1, jq, l0)
            out_ref[...] += _dot(vt_ref[:, r1:r1 + s_rows], a_ref[r1:r1 + s_rows, :])

    @pl.when(g % 2 == 0)
    def _():
        run(s0_ref, s1_ref)

    @pl.when(g % 2 == 1)
    def _():
        run(s1_ref, s0_ref)

    @pl.when((g > 0) & (c_gate == n_chunks - 1))
    def _():
        rows = g2_ref.shape[0]
        r_n = min(max(rows, MXU_DIM), tb)
        for r0 in range(0, tb, r_n):
            peer = out_ref[:, r0:r0 + r_n].T
            gate2 = g2_ref[...] if rows in (1, r_n) else jnp.concatenate([g2_ref[...]] * (r_n // rows), axis=0)
            xo = x1_ref[r0:r0 + r_n, :] + gate2 * peer
            y_ref[r0:r0 + r_n, :] = _rms(xo, fg_ref[...]) if final_norm else xo


def _peer_experts(h2t, kc, e1, r2, e2, u_bf, vt_bf, x1_flat, gate2, gate2_spec, final_g, final_norm):
    d, t = h2t.shape
    n_heads, nk, _ = kc.shape
    ne = u_bf.shape[0]
    tb = min(PEER_TOKENS, t)
    ec = PEER_EXPERTS
    nc = ne // ec
    n = (t // tb) * nc
    gate_step = lambda g: jnp.maximum(g - 1, 0)
    chunk_tab = pl.BlockSpec((n_heads, ec // nk, tb), lambda g: (0, gate_step(g) % nc, gate_step(g) // nc))
    token_tab = pl.BlockSpec((n_heads * nk, tb), lambda g: (0, gate_step(g) // nc),
                             pipeline_mode=pl.Buffered(1))
    return pl.pallas_call(
        functools.partial(_peer_kernel, n_heads=n_heads, nk=nk, n_chunks=nc, final_norm=final_norm),
        grid=(n + 1,),
        in_specs=[pl.BlockSpec((d, tb), lambda g: (0, jnp.minimum(g, n - 1) // nc),
                               pipeline_mode=pl.Buffered(1)),
                  chunk_tab, chunk_tab, token_tab, token_tab,
                  pl.BlockSpec((ec, d), lambda g: (jnp.minimum(g, n - 1) % nc, 0)),
                  pl.BlockSpec((None, d, ec), lambda g: (gate_step(g) % nc, 0, 0)),
                  pl.BlockSpec((tb, d), lambda g: (gate_step(g) // nc, 0), pipeline_mode=pl.Buffered(1)),
                  gate2_spec(lambda g: gate_step(g) // nc), _const_spec(final_g.shape)],
        out_specs=pl.BlockSpec((tb, d), lambda g: (gate_step(g) // nc, 0)),
        out_shape=jax.ShapeDtypeStruct((t, d), F32),
        scratch_shapes=[pltpu.VMEM((PEER_I_GROUP, 2 * n_heads, 2 * SUBLANES, tb), BF16),
                        pltpu.VMEM((ec, tb), BF16),
                        pltpu.VMEM((ec, tb), BF16),
                        pltpu.VMEM((ec, tb), BF16),
                        pltpu.VMEM((d, tb), F32)],
        compiler_params=_params(("arbitrary",), 56),
    )(h2t, kc, e1, r2, e2, u_bf, vt_bf, x1_flat, gate2, final_g)


def _block_diag(w):
    heads, hd, _ = w.shape
    per = MXU_DIM // hd
    wg = w.reshape(heads // per, per, hd, hd)
    eye = jnp.eye(per, dtype=w.dtype)
    return jnp.einsum("gpij,pq->gpiqj", wg, eye).reshape(heads // per, MXU_DIM, MXU_DIM).astype(BF16)


def kernel(x_prompt, x_sample, c_prompt, c_sample, state_lru_h, state_lru_conv, state_sconv, w_ada, b_ada, norm1_g, norm2_g, w_in, lru_conv_w, lru_conv_b, lru_wa, lru_ba, lru_wx, lru_bx, lru_lambda, sconv_w, gnorm_lru_g, gnorm_sc_g, w_out, peer_wq, peer_sub_keys, peer_u, peer_v, final_g):
    bp, seq, d = x_prompt.shape
    bs, sseq, _ = x_sample.shape
    depth = w_ada.shape[0]

    xp = x_prompt.reshape(bp * seq, d)
    xs = jnp.swapaxes(x_sample, 0, 1).reshape(sseq * bs, d)
    n_c = bp + bs
    pad = (-n_c) % SUBLANES
    c_all = jnp.concatenate([c_prompt, c_sample, jnp.zeros((pad, d), F32)], axis=0)
    fg = final_g.reshape(1, d)

    mod_p_spec = pl.BlockSpec((6, None, 1, d), lambda i: (0, i // (seq // PREP_TILE), 0, 0))
    mod_s_spec = _const_spec((6, bs, d))
    blocks_per_seq = seq // PEER_TOKENS
    gate2_p_spec = lambda blk: pl.BlockSpec((None, None, 1, d), lambda g: (5, blk(g) // blocks_per_seq, 0, 0))
    gate2_s_spec = lambda blk: pl.BlockSpec((None, bs, d), lambda g: (5, 0, 0), pipeline_mode=pl.Buffered(1))

    outs = [[] for _ in range(6)]
    for l in range(depth):
        row = lambda a: a[l].reshape(1, -1)
        lw = dict(
            norm1_g=row(norm1_g), norm2_g=row(norm2_g), w_in=w_in[l].astype(BF16),
            lru_conv_w=lru_conv_w[l], lru_conv_b=row(lru_conv_b),
            wa_bd=_block_diag(lru_wa[l]), lru_ba=row(lru_ba),
            wx_bd=_block_diag(lru_wx[l]), lru_bx=row(lru_bx), lru_lambda=row(lru_lambda),
            sconv_w=sconv_w[l], gnorm_lru_g=row(gnorm_lru_g), gnorm_sc_g=row(gnorm_sc_g),
            w_out=w_out[l].astype(BF16), peer_wq=peer_wq[l].astype(BF16),
            peer_sub_keys=peer_sub_keys[l].astype(BF16))
        u_bf = peer_u[l].astype(BF16)
        ne = peer_v.shape[1]
        vt_bf = jnp.swapaxes(peer_v[l].reshape(ne // PEER_EXPERTS, PEER_EXPERTS, d), 1, 2).astype(BF16)

        mod = _adaln(c_all, w_ada[l], b_ada[l])
        mod_p = jnp.swapaxes(mod[:bp].reshape(bp, 6, d), 0, 1).reshape(6, bp, 1, d)
        mod_s = jnp.swapaxes(mod[bp:n_c].reshape(bs, 6, d), 0, 1)

        xp1, hl, lb, sb = _mixer_prompt(xp, mod_p, lw, bp, seq)
        h0 = state_lru_h[l]
        lb0 = jnp.swapaxes(state_lru_conv[l], 0, 1)
        sb0 = jnp.swapaxes(state_sconv[l], 0, 1)
        xs1, hs, lbs, sbs = _mixer_sample(xs, mod_s, lw, h0, lb0, sb0, bs)

        last = l == depth - 1
        new_x = []
        for x1, mod_g, spec, rows, g2_spec in ((xp1, mod_p, mod_p_spec, PREP_TILE, gate2_p_spec),
                                               (xs1, mod_s, mod_s_spec, bs, gate2_s_spec)):
            new_x.append(_peer_experts(*_peer_prep(x1, mod_g, spec, rows, lw), u_bf, vt_bf,
                                       x1, mod_g, g2_spec, fg, last))
        xp, xs = new_x

        nl = lru_conv_w.shape[1] - 1
        ns = sconv_w.shape[1] - 1
        outs[0].append(hl[:, SUBLANES - 1, :])
        outs[1].append(lb[:, SUBLANES - nl:, :])
        outs[2].append(sb[:, SUBLANES - ns:, :])
        outs[3].append(hs)
        outs[4].append(jnp.swapaxes(lbs, 0, 1))
        outs[5].append(jnp.swapaxes(sbs, 0, 1))

    y_prompt = xp.reshape(bp, seq, d)
    y_sample = jnp.swapaxes(xs.reshape(sseq, bs, d), 0, 1)
    return (y_prompt, y_sample) + tuple(jnp.stack(o) for o in outs)
```
